```python
import math
import jax, jax.numpy as jnp
from jax import lax
import numpy as np

D_MODEL = 1024
BATCH = 32
SEQ = 2048
DEPTH = 1

SSD_D_INNER = 1024
SSD_HEAD_DIM = 64
SSD_HEADS = SSD_D_INNER // SSD_HEAD_DIM
SSD_GROUPS = 2
SSD_D_STATE = 128
SSD_CONV = 4
SSD_CHUNK = 128
SSD_CONV_DIM = SSD_D_INNER + 2 * SSD_GROUPS * SSD_D_STATE
SSD_NORM_EPS = 1e-5
DA_HEADS = 8
DA_HEAD_DIM = 64
DA_WIDTH = DA_HEADS * 2 * DA_HEAD_DIM
ROPE_THETA = 500000.0
ROPE_DIM = DA_HEAD_DIM // 4
Q_BLOCK = 128
SUBLN_EPS = 1e-5
D_FF = 2816
FFN_CONV = 3
N_BRANCH = 2
NORM_EPS = 1e-6
IN_SPLITS = (
    SSD_D_INNER,
    SSD_D_INNER + SSD_CONV_DIM,
    SSD_D_INNER + SSD_CONV_DIM + SSD_HEADS,
    SSD_D_INNER + SSD_CONV_DIM + SSD_HEADS + DA_WIDTH,
    SSD_D_INNER + SSD_CONV_DIM + SSD_HEADS + 2 * DA_WIDTH,
    SSD_D_INNER + SSD_CONV_DIM + SSD_HEADS + 3 * DA_WIDTH,
)
IN_WIDTH = SSD_D_INNER + SSD_CONV_DIM + SSD_HEADS + 3 * DA_WIDTH + N_BRANCH * D_MODEL

kernel_name = 'hybrid_ssd_diffattn_gated_convffn'


def rms_norm(x, w, eps=NORM_EPS):
    xf = x.astype(jnp.float32)
    y = xf * lax.rsqrt(jnp.mean(xf * xf, axis=-1, keepdims=True) + eps)
    return (y * w.astype(jnp.float32)).astype(x.dtype)


def causal_dwconv(x, w, b):
    k, ch = w.shape
    y = lax.conv_general_dilated(x, w[:, None, :].astype(x.dtype), window_strides=(1,),
                                 padding=[(k - 1, 0)], dimension_numbers=('NWC', 'WIO', 'NWC'),
                                 feature_group_count=ch)
    return y + b.astype(x.dtype)


def rope_tables(seq):
    pos = jnp.arange(seq, dtype=jnp.float32)
    inv_freq = jnp.power(ROPE_THETA, -jnp.arange(0, ROPE_DIM, 2, dtype=jnp.float32) / ROPE_DIM)
    ang = pos[:, None] * inv_freq[None, :]
    return jnp.cos(ang), jnp.sin(ang)


def partial_rope(t, cos, sin):
    half = ROPE_DIM // 2
    r1 = t[..., :half].astype(jnp.float32)
    r2 = t[..., half:ROPE_DIM].astype(jnp.float32)
    cs, sn = cos[None, :, None, :], sin[None, :, None, :]
    rot = jnp.concatenate([r1 * cs - r2 * sn, r1 * sn + r2 * cs], axis=-1).astype(t.dtype)
    return jnp.concatenate([rot, t[..., ROPE_DIM:]], axis=-1)


def ssd_chunked(xs, dt, a, bm, cm):
    bsz, seq, nh, hp = xs.shape
    g, n = bm.shape[2], bm.shape[3]
    r = nh // g
    nc, l = seq // SSD_CHUNK, SSD_CHUNK
    xd = (xs.astype(jnp.float32) * dt[..., None]).reshape(bsz, nc, l, g, r, hp)
    la = (dt * a).reshape(bsz, nc, l, g, r)
    bc = bm.astype(jnp.float32).reshape(bsz, nc, l, g, n)
    cc = cm.astype(jnp.float32).reshape(bsz, nc, l, g, n)
    a_cum = jnp.cumsum(la, axis=2)
    seg = a_cum[:, :, :, None] - a_cum[:, :, None, :]
    causal = jnp.tril(jnp.ones((l, l), dtype=bool))[:, :, None, None]
    decay = jnp.exp(jnp.where(causal, seg, -jnp.inf))
    cb = jnp.einsum('bclgn,bcsgn->bclsg', cc, bc)
    y_diag = jnp.einsum('bclsgr,bcsgrp->bclgrp', cb[..., None] * decay, xd)
    decay_to_end = jnp.exp(a_cum[:, :, -1:] - a_cum)
    states = jnp.einsum('bclgn,bclgrp->bcgrpn', bc, xd * decay_to_end[..., None])
    chunk_decay = jnp.exp(a_cum[:, :, -1])

    def step(carry, inp):
        st, dec = inp
        return carry * dec[..., None, None] + st, carry

    init = jnp.zeros((bsz, g, r, hp, n), jnp.float32)
    _, prev = lax.scan(step, init, (jnp.swapaxes(states, 0, 1), jnp.swapaxes(chunk_decay, 0, 1)))
    prev = jnp.swapaxes(prev, 0, 1)
    y_off = jnp.einsum('bclgn,bcgrpn->bclgrp', cc, prev) * jnp.exp(a_cum)[..., None]
    return (y_diag + y_off).reshape(bsz, seq, nh, hp)


def ssd_branch(z, xbc, dt_raw, conv_w, conv_b, dt_bias, a_log, d_skip, norm_w):
    bsz, seq, _ = z.shape
    xbc = jax.nn.silu(causal_dwconv(xbc, conv_w, conv_b))
    xs, bm, cm = jnp.split(xbc, [SSD_D_INNER, SSD_D_INNER + SSD_GROUPS * SSD_D_STATE], axis=-1)
    xs = xs.reshape(bsz, seq, SSD_HEADS, SSD_HEAD_DIM)
    bm = bm.reshape(bsz, seq, SSD_GROUPS, SSD_D_STATE)
    cm = cm.reshape(bsz, seq, SSD_GROUPS, SSD_D_STATE)
    dt = jax.nn.softplus(dt_raw.astype(jnp.float32) + dt_bias.astype(jnp.float32))
    a = -jnp.exp(a_log.astype(jnp.float32))
    y = ssd_chunked(xs, dt, a, bm, cm)
    y = y + d_skip.astype(jnp.float32)[:, None] * xs.astype(jnp.float32)
    y = y.reshape(bsz, seq, SSD_D_INNER) * jax.nn.silu(z.astype(jnp.float32))
    yg = y.reshape(bsz, seq, SSD_GROUPS, SSD_D_INNER // SSD_GROUPS)
    yg = yg * lax.rsqrt(jnp.mean(yg * yg, axis=-1, keepdims=True) + SSD_NORM_EPS)
    return (yg.reshape(bsz, seq, SSD_D_INNER) * norm_w.astype(jnp.float32)).astype(z.dtype)


def diff_attention_branch(q, k, v, lq1, lk1, lq2, lk2, subln_w, cos, sin, lam_init):
    bsz, seq, _ = q.shape
    q = partial_rope(q.reshape(bsz, seq, 2 * DA_HEADS, DA_HEAD_DIM), cos, sin).transpose(0, 2, 1, 3)
    k = partial_rope(k.reshape(bsz, seq, 2 * DA_HEADS, DA_HEAD_DIM), cos, sin).transpose(0, 2, 1, 3)
    v = v.reshape(bsz, seq, DA_HEADS, 2 * DA_HEAD_DIM).transpose(0, 2, 1, 3)
    f32 = jnp.float32
    lam = (jnp.exp(jnp.sum(lq1.astype(f32) * lk1.astype(f32)))
           - jnp.exp(jnp.sum(lq2.astype(f32) * lk2.astype(f32))) + lam_init)
    scale = DA_HEAD_DIM ** -0.5
    outs = []
    for i in range(seq // Q_BLOCK):
        q0, q1 = i * Q_BLOCK, (i + 1) * Q_BLOCK
        s = jnp.einsum('bhqd,bhkd->bhqk', q[:, :, q0:q1], k[:, :, :q1]).astype(f32) * scale
        mask = jnp.arange(q1)[None, :] <= jnp.arange(q0, q1)[:, None]
        p = jax.nn.softmax(jnp.where(mask, s, -jnp.inf), axis=-1)
        p = p.reshape(bsz, DA_HEADS, 2, Q_BLOCK, q1)
        attn = (p[:, :, 0] - lam * p[:, :, 1]).astype(v.dtype)
        outs.append(jnp.einsum('bhqk,bhkd->bhqd', attn, v[:, :, :q1]))
    o = jnp.concatenate(outs, axis=2)
    o = rms_norm(o, subln_w, eps=SUBLN_EPS) * (1.0 - lam_init)
    return o.transpose(0, 2, 1, 3).reshape(bsz, seq, DA_WIDTH)


def setup_inputs(seed: int = 0) -> dict:
    key = jax.random.key(seed)
    ks = jax.random.split(key, 32)
    nrm = lambda k, shape, s: jax.random.normal(k, shape, jnp.float32) * s
    L = DEPTH
    dt0 = jnp.exp(jax.random.uniform(ks[7], (L, SSD_HEADS)) * (math.log(0.1) - math.log(0.001)) + math.log(0.001))
    return {
        'x': nrm(ks[0], (BATCH, SEQ, D_MODEL), 1.0),
        'c': nrm(ks[1], (BATCH, D_MODEL), 1.0),
        'w_ada': nrm(ks[2], (L, D_MODEL, 6 * D_MODEL), D_MODEL ** -0.5),
        'b_ada': nrm(ks[3], (L, 6 * D_MODEL), 0.02),
        'pre_norm1_w': 1.0 + nrm(ks[4], (L, D_MODEL), 0.05),
        'w_in': nrm(ks[5], (L, D_MODEL, IN_WIDTH), D_MODEL ** -0.5),
        'conv_ssd_w': nrm(ks[6], (L, SSD_CONV, SSD_CONV_DIM), SSD_CONV ** -0.5),
        'conv_ssd_b': nrm(ks[8], (L, SSD_CONV_DIM), 0.02),
        'dt_bias': dt0 + jnp.log(-jnp.expm1(-dt0)),
        'a_log': jnp.log(jax.random.uniform(ks[9], (L, SSD_HEADS), jnp.float32, 1.0, 16.0)),
        'd_skip': 1.0 + nrm(ks[10], (L, SSD_HEADS), 0.1),
        'ssd_norm_w': 1.0 + nrm(ks[11], (L, SSD_D_INNER), 0.05),
        'w_ssd_o': nrm(ks[12], (L, SSD_D_INNER, D_MODEL), SSD_D_INNER ** -0.5),
        'lambda_q1': nrm(ks[13], (L, DA_HEAD_DIM), 0.1),
        'lambda_k1': nrm(ks[14], (L, DA_HEAD_DIM), 0.1),
        'lambda_q2': nrm(ks[15], (L, DA_HEAD_DIM), 0.1),
        'lambda_k2': nrm(ks[16], (L, DA_HEAD_DIM), 0.1),
        'subln_w': 1.0 + nrm(ks[17], (L, 2 * DA_HEAD_DIM), 0.05),
        'w_attn_o': nrm(ks[18], (L, DA_WIDTH, D_MODEL), DA_WIDTH ** -0.5),
        'w_out': nrm(ks[19], (L, D_MODEL, D_MODEL), D_MODEL ** -0.5),
        'post_norm1_w': 1.0 + nrm(ks[20], (L, D_MODEL), 0.05),
        'pre_norm2_w': 1.0 + nrm(ks[21], (L, D_MODEL), 0.05),
        'w_up': nrm(ks[22], (L, D_MODEL, 2 * D_FF), D_MODEL ** -0.5),
        'conv_ffn_w': nrm(ks[23], (L, FFN_CONV, 2 * D_FF), FFN_CONV ** -0.5),
        'conv_ffn_b': nrm(ks[24], (L, 2 * D_FF), 0.02),
        'w_down': nrm(ks[25], (L, D_FF, D_MODEL), D_FF ** -0.5),
        'post_norm2_w': 1.0 + nrm(ks[26], (L, D_MODEL), 0.05),
    }


def reference(x, c, w_ada, b_ada, pre_norm1_w, w_in, conv_ssd_w, conv_ssd_b, dt_bias, a_log,
              d_skip, ssd_norm_w, w_ssd_o, lambda_q1, lambda_k1, lambda_q2, lambda_k2, subln_w,
              w_attn_o, w_out, post_norm1_w, pre_norm2_w, w_up, conv_ffn_w, conv_ffn_b, w_down,
              post_norm2_w):
    bsz, seq, _ = x.shape
    cos, sin = rope_tables(seq)
    for layer in range(DEPTH):
        lam_init = 0.8 - 0.6 * math.exp(-0.3 * layer)
        mod = jax.nn.silu(c) @ w_ada[layer] + b_ada[layer]
        sh1, sc1, g1, sh2, sc2, g2 = [m[:, None, :] for m in jnp.split(mod, 6, axis=-1)]
        h = rms_norm(x, pre_norm1_w[layer]) * (1.0 + sc1) + sh1
        proj = h @ w_in[layer]
        z, xbc, dt_raw, q, k, v, gates = jnp.split(proj, IN_SPLITS, axis=-1)
        y_ssd = ssd_branch(z, xbc, dt_raw, conv_ssd_w[layer], conv_ssd_b[layer], dt_bias[layer],
                           a_log[layer], d_skip[layer], ssd_norm_w[layer])
        y_att = diff_attention_branch(q, k, v, lambda_q1[layer], lambda_k1[layer], lambda_q2[layer],
                                      lambda_k2[layer], subln_w[layer], cos, sin, lam_init)
        gate_ssd, gate_att = jnp.split(jax.nn.sigmoid(gates), N_BRANCH, axis=-1)
        merged = gate_ssd * (y_ssd @ w_ssd_o[layer]) + gate_att * (y_att @ w_attn_o[layer])
        mix = merged @ w_out[layer]
        x = x + g1 * rms_norm(mix, post_norm1_w[layer])
        h2 = rms_norm(x, pre_norm2_w[layer]) * (1.0 + sc2) + sh2
        u = causal_dwconv(h2 @ w_up[layer], conv_ffn_w[layer], conv_ffn_b[layer])
        u_gate, u_val = jnp.split(u, 2, axis=-1)
        f = (jax.nn.gelu(u_gate, approximate=True) * u_val) @ w_down[layer]
        x = x + g2 * rms_norm(f, post_norm2_w[layer])
    return x
```

```python
import functools
import math

import jax
import jax.numpy as jnp
from jax import lax
from jax.experimental import pallas as pl
from jax.experimental.pallas import tpu as pltpu

F32 = jnp.float32
BF16 = jnp.bfloat16

D_MODEL = 1024
SSD_D_INNER = 1024
SSD_HEAD_DIM = 64
SSD_HEADS = SSD_D_INNER // SSD_HEAD_DIM
SSD_GROUPS = 2
SSD_D_STATE = 128
SSD_CONV = 4
SSD_CHUNK = 128
SSD_CONV_DIM = SSD_D_INNER + 2 * SSD_GROUPS * SSD_D_STATE
SSD_NORM_EPS = 1e-5
DA_HEADS = 8
DA_HEAD_DIM = 64
DA_WIDTH = DA_HEADS * 2 * DA_HEAD_DIM
ROPE_THETA = 500000.0
ROPE_DIM = DA_HEAD_DIM // 4
SUBLN_EPS = 1e-5
D_FF = 2816
FFN_CONV = 3
NORM_EPS = 1e-6
LAM_INIT = 0.8 - 0.6 * math.exp(-0.3 * 0)

LANES = 128
SUBLANES = 8
VMEM_LIMIT_BYTES = 56 * 1024 * 1024

ROW_TILE = 512
COL_CHUNK = 512
FF_CHUNK = 256
ATT_BLOCK = 256


def _silu(x):
    return x * jax.nn.sigmoid(x)


def _rms(x, eps):
    return x * lax.rsqrt(jnp.mean(x * x, axis=-1, keepdims=True) + eps)


def _resident(shape):
    nd = len(shape)
    return pl.BlockSpec(shape, lambda *_: (0,) * nd, pipeline_mode=pl.Buffered(1))


def _causal_dwconv_tile(u, prev8, w, bias):
    taps = w.shape[0]
    y = u * w[taps - 1:taps] + bias
    for k in range(1, taps):
        y = y + pltpu.roll(u, k, axis=0) * w[taps - 1 - k:taps - k]
    top_rows = 2 * SUBLANES
    top = u[:top_rows]
    prev = jnp.concatenate([jnp.zeros_like(prev8), prev8], axis=0)
    row = lax.broadcasted_iota(jnp.int32, top.shape, 0)
    ytop = top * w[taps - 1:taps] + bias
    for k in range(1, taps):
        shifted = jnp.where(row < k, pltpu.roll(prev, k, axis=0), pltpu.roll(top, k, axis=0))
        ytop = ytop + shifted * w[taps - 1 - k:taps - k]
    return jnp.concatenate([ytop, y[top_rows:]], axis=0)


def _ada_kernel(c_ref, w_ref, b_ref, o_ref):
    c = c_ref[...]
    o_ref[0] = jnp.dot(_silu(c), w_ref[...], preferred_element_type=F32,
                       precision=lax.Precision.HIGHEST) + b_ref[0]


def _ada(c, w_ada, b_ada):
    bsz, d = c.shape
    n_mod = w_ada.shape[1] // d
    return pl.pallas_call(
        _ada_kernel,
        grid=(n_mod,),
        in_specs=[
            pl.BlockSpec((bsz, d), lambda j: (0, 0)),
            pl.BlockSpec((d, d), lambda j: (0, j)),
            pl.BlockSpec((1, 1, d), lambda j: (j, 0, 0)),
        ],
        out_specs=pl.BlockSpec((1, bsz, d), lambda j: (j, 0, 0)),
        out_shape=jax.ShapeDtypeStruct((n_mod, bsz, d), F32),
        compiler_params=pltpu.CompilerParams(dimension_semantics=("arbitrary",),
                                             vmem_limit_bytes=VMEM_LIMIT_BYTES),
        name="ada",
    )(c, w_ada, b_ada.reshape(n_mod, 1, d))


_OFF_Z = 0
_OFF_XBC = _OFF_Z + SSD_D_INNER
_OFF_Q = _OFF_XBC + SSD_CONV_DIM
_OFF_K = _OFF_Q + DA_WIDTH
_OFF_V = _OFF_K + DA_WIDTH
_OFF_G = _OFF_V + DA_WIDTH
_OFF_DT = _OFF_G + 2 * D_MODEL
_W_COLS = _OFF_DT + LANES


def _inproj_kernel(tiles_per_seq, x_ref, sc_ref, sh_ref, nw_ref, w_ref, cosf_ref, sina_ref,
                   sinb_ref, cw_ref, cb_ref, dtb_ref,
                   z_ref, xbc_ref, q_ref, k_ref, v_ref, g_ref, dt_ref, carry_ref):
    rows = x_ref.shape[0]
    first = (pl.program_id(0) % tiles_per_seq) == 0
    x = x_ref[...]
    h = (_rms(x, NORM_EPS) * nw_ref[...]) * (1.0 + sc_ref[...]) + sh_ref[...]
    hb = h.astype(BF16)

    def mm(c0, width):
        return jnp.dot(hb, w_ref[:, c0:c0 + width], preferred_element_type=F32)

    for c in range(0, SSD_D_INNER, COL_CHUNK):
        z_ref[:, c:c + COL_CHUNK] = mm(_OFF_Z + c, COL_CHUNK).astype(BF16)

    for c in range(0, SSD_CONV_DIM, COL_CHUNK):
        u = mm(_OFF_XBC + c, COL_CHUNK)
        prev8 = jnp.where(first, 0.0, carry_ref[:, c:c + COL_CHUNK])
        y = _causal_dwconv_tile(u, prev8, cw_ref[:, c:c + COL_CHUNK], cb_ref[:, c:c + COL_CHUNK])
        carry_ref[:, c:c + COL_CHUNK] = u[rows - SUBLANES:]
        xbc_ref[:, c:c + COL_CHUNK] = _silu(y).astype(BF16)

    cosf = cosf_ref[...]
    sina = sina_ref[...]
    sinb = sinb_ref[...]
    half = ROPE_DIM // 2
    for off, out_ref, scale in ((_OFF_Q, q_ref, DA_HEAD_DIM ** -0.5), (_OFF_K, k_ref, 1.0)):
        for c in range(0, DA_WIDTH, COL_CHUNK):
            acc = mm(off + c, COL_CHUNK)
            for s in range(0, COL_CHUNK, LANES):
                t = acc[:, s:s + LANES]
                r = (t * cosf + pltpu.roll(t, LANES - half, axis=1) * sina
                     + pltpu.roll(t, half, axis=1) * sinb)
                out_ref[:, c + s:c + s + LANES] = (r * scale).astype(BF16)

    for c in range(0, DA_WIDTH, COL_CHUNK):
        v_ref[:, c:c + COL_CHUNK] = mm(_OFF_V + c, COL_CHUNK).astype(BF16)

    for c in range(0, 2 * D_MODEL, COL_CHUNK):
        g_ref[:, c:c + COL_CHUNK] = jax.nn.sigmoid(mm(_OFF_G + c, COL_CHUNK)).astype(BF16)

    dt_raw = mm(_OFF_DT, LANES) + dtb_ref[...]
    dt_ref[...] = jnp.maximum(dt_raw, 0.0) + jnp.log1p(jnp.exp(-jnp.abs(dt_raw)))


def _rope_tables(seq):
    pos = jnp.arange(seq, dtype=F32)
    inv_freq = jnp.power(ROPE_THETA, -jnp.arange(0, ROPE_DIM, 2, dtype=F32) / ROPE_DIM)
    ang = pos[:, None] * inv_freq[None, :]
    cos, sin = jnp.cos(ang), jnp.sin(ang)
    half = ROPE_DIM // 2
    d = jnp.arange(LANES) % DA_HEAD_DIM
    idx = d % half
    cosf = jnp.where(d[None, :] < ROPE_DIM, cos[:, idx], 1.0)
    sina = jnp.where(d[None, :] < half, -sin[:, idx], 0.0)
    sinb = jnp.where((d[None, :] >= half) & (d[None, :] < ROPE_DIM), sin[:, idx], 0.0)
    return cosf.astype(F32), sina.astype(F32), sinb.astype(F32)


def _inproj(x2, mod, pre_norm_w, w_cat, conv_w, conv_b, dt_bias_pad, seq):
    tokens, d = x2.shape
    tiles_per_seq = seq // ROW_TILE
    cosf, sina, sinb = _rope_tables(seq)

    def row_spec(width):
        return pl.BlockSpec((ROW_TILE, width), lambda i: (i, 0))

    def mod_spec(which):
        return pl.BlockSpec((None, None, 1, d), lambda i: (which, i // tiles_per_seq, 0, 0))

    def rope_spec():
        return pl.BlockSpec((ROW_TILE, LANES), lambda i: (i % tiles_per_seq, 0))

    out_shapes = (
        jax.ShapeDtypeStruct((tokens, SSD_D_INNER), BF16),
        jax.ShapeDtypeStruct((tokens, SSD_CONV_DIM), BF16),
        jax.ShapeDtypeStruct((tokens, DA_WIDTH), BF16),
        jax.ShapeDtypeStruct((tokens, DA_WIDTH), BF16),
        jax.ShapeDtypeStruct((tokens, DA_WIDTH), BF16),
        jax.ShapeDtypeStruct((tokens, 2 * D_MODEL), BF16),
        jax.ShapeDtypeStruct((tokens, LANES), F32),
    )
    return pl.pallas_call(
        functools.partial(_inproj_kernel, tiles_per_seq),
        grid=(tokens // ROW_TILE,),
        in_specs=[
            row_spec(d), mod_spec(1), mod_spec(0), _resident((1, d)), _resident((d, _W_COLS)),
            rope_spec(), rope_spec(), rope_spec(),
            _resident((SSD_CONV, SSD_CONV_DIM)), _resident((1, SSD_CONV_DIM)), _resident((1, LANES)),
        ],
        out_specs=[row_spec(s.shape[1]) for s in out_shapes],
        out_shape=out_shapes,
        scratch_shapes=[pltpu.VMEM((SUBLANES, SSD_CONV_DIM), F32)],
        compiler_params=pltpu.CompilerParams(dimension_semantics=("arbitrary",),
                                             vmem_limit_bytes=VMEM_LIMIT_BYTES),
        name="inproj",
    )(x2, mod, mod, pre_norm_w, w_cat, cosf, sina, sinb, conv_w, conv_b, dt_bias_pad)


def _split3(a):
    hi = a.astype(BF16)
    r = a - hi.astype(F32)
    mid = r.astype(BF16)
    lo = (r - mid.astype(F32)).astype(BF16)
    return hi, mid, lo


def _ssd_kernel(xbc_ref, dt_ref, z_ref, a_ref, dskip_ref, nw_ref, expand_ref, o_ref, state_ref):
    l = SSD_CHUNK
    n = SSD_D_STATE
    gw = SSD_D_INNER // SSD_GROUPS

    @pl.when(pl.program_id(1) == 0)
    def _():
        state_ref[...] = jnp.zeros_like(state_ref)

    xbc = xbc_ref[...]
    xs = xbc[:, :SSD_D_INNER].astype(F32)
    dt = dt_ref[...]
    la = dt * a_ref[...]
    row = lax.broadcasted_iota(jnp.int32, (l, l), 0)
    col = lax.broadcasted_iota(jnp.int32, (l, l), 1)
    causal = row >= col
    tril = jnp.where(causal, 1.0, 0.0).astype(BF16)
    expand = expand_ref[...]

    acum = sum(jnp.dot(tril, part, preferred_element_type=F32) for part in _split3(la))
    acum_t = acum.T
    alast = acum[l - 1:l, :]
    dte = jnp.exp(alast - acum)
    dt_x = jnp.dot(dt.astype(BF16), expand, preferred_element_type=F32)
    w_x = jnp.dot((dt * dte).astype(BF16), expand, preferred_element_type=F32)
    cdec = jnp.broadcast_to(jnp.exp(alast), (SUBLANES, LANES))
    cdec_x = sum(jnp.dot(part, expand, preferred_element_type=F32) for part in _split3(cdec))[0:1]

    xd = (xs * dt_x).astype(BF16)
    xw = (xs * w_x).astype(BF16)
    lane = lax.broadcasted_iota(jnp.int32, (l, LANES), 1)

    pairs = []
    for g in range(SSD_GROUPS):
        bm = xbc[:, SSD_D_INNER + g * n:SSD_D_INNER + (g + 1) * n]
        cm = xbc[:, SSD_D_INNER + (SSD_GROUPS + g) * n:SSD_D_INNER + (SSD_GROUPS + g + 1) * n]
        cb = lax.dot_general(cm, bm, (((1,), (1,)), ((), ())), preferred_element_type=F32)
        cmf = cm.astype(F32)
        state = state_ref[g]
        state_b = state.astype(BF16)
        heads_per_group = SSD_HEADS // SSD_GROUPS
        for pr in range(heads_per_group // 2):
            res = []
            for sub in range(2):
                h = g * heads_per_group + 2 * pr + sub
                a_col = acum[:, h:h + 1]
                seg = a_col - acum_t[h:h + 1, :]
                decay = jnp.exp(jnp.where(causal, seg, -jnp.inf))
                lhs = jnp.concatenate([(cb * decay).astype(BF16),
                                       (cmf * jnp.exp(a_col)).astype(BF16)], axis=1)
                c0 = g * gw + pr * LANES
                rhs = jnp.concatenate([xd[:, c0:c0 + LANES],
                                       state_b[:, pr * LANES:(pr + 1) * LANES]], axis=0)
                res.append(jnp.dot(lhs, rhs, preferred_element_type=F32))
            pairs.append(jnp.where(lane < SSD_HEAD_DIM, res[0], res[1]))
        bm_t = bm.astype(F32).T.astype(BF16)
        new = jnp.dot(bm_t, xw[:, g * gw:(g + 1) * gw], preferred_element_type=F32)
        state_ref[g] = state * cdec_x[:, g * gw:(g + 1) * gw] + new

    y = jnp.concatenate(pairs, axis=1)
    y = y + dskip_ref[...] * xs
    y = y * _silu(z_ref[...].astype(F32))
    y = jnp.concatenate([_rms(y[:, g * gw:(g + 1) * gw], SSD_NORM_EPS) for g in range(SSD_GROUPS)],
                        axis=1)
    o_ref[...] = (y * nw_ref[...]).astype(BF16)


def _ssd(xbc, dt, z, a_row, dskip_x, norm_w, bsz, seq):
    tokens = xbc.shape[0]
    nc = seq // SSD_CHUNK
    head = jnp.arange(LANES)[:, None]
    chan = jnp.arange(SSD_D_INNER)[None, :] // SSD_HEAD_DIM
    expand = (head == chan).astype(BF16)

    def row_spec(width):
        return pl.BlockSpec((SSD_CHUNK, width), lambda b, c: (b * nc + c, 0))

    return pl.pallas_call(
        _ssd_kernel,
        grid=(bsz, nc),
        in_specs=[
            row_spec(SSD_CONV_DIM), row_spec(LANES), row_spec(SSD_D_INNER),
            _resident((1, LANES)), _resident((1, SSD_D_INNER)), _resident((1, SSD_D_INNER)),
            _resident((LANES, SSD_D_INNER)),
        ],
        out_specs=row_spec(SSD_D_INNER),
        out_shape=jax.ShapeDtypeStruct((tokens, SSD_D_INNER), BF16),
        scratch_shapes=[pltpu.VMEM((SSD_GROUPS, SSD_D_STATE, SSD_D_INNER // SSD_GROUPS), F32)],
        compiler_params=pltpu.CompilerParams(dimension_semantics=("arbitrary", "arbitrary"),
                                             vmem_limit_bytes=VMEM_LIMIT_BYTES),
        name="ssd",
    )(xbc, dt, z, a_row, dskip_x, norm_w, expand)


def _attn_kernel(lq1_ref, lk1_ref, lq2_ref, lk2_ref, sw_ref, q_ref, k_ref, v_ref, o_ref):
    tq = q_ref.shape[0]
    i = pl.program_id(2)
    q = q_ref[...]
    lane = lax.broadcasted_iota(jnp.int32, q.shape, 1)
    zero = jnp.zeros_like(q)
    qs = jnp.concatenate([jnp.where(lane < DA_HEAD_DIM, q, zero),
                          jnp.where(lane >= DA_HEAD_DIM, q, zero)], axis=0)

    def step(j, carry, diagonal):
        m, l, acc = carry
        start = pl.multiple_of(j * tq, tq)
        kb = k_ref[pl.ds(start, tq), :]
        vb = v_ref[pl.ds(start, tq), :]
        s = lax.dot_general(qs, kb, (((1,), (1,)), ((), ())), preferred_element_type=F32)
        if diagonal:
            rq = lax.broadcasted_iota(jnp.int32, s.shape, 0)
            rq = jnp.where(rq >= tq, rq - tq, rq)
            ck = lax.broadcasted_iota(jnp.int32, s.shape, 1)
            s = jnp.where(ck <= rq, s, -jnp.inf)
        m_new = jnp.maximum(m, jnp.max(s, axis=-1, keepdims=True))
        alpha = jnp.exp(m - m_new)
        p = jnp.exp(s - m_new)
        l = alpha * l + jnp.sum(p, axis=-1, keepdims=True)
        acc = alpha * acc + jnp.dot(p.astype(BF16), vb, preferred_element_type=F32)
        return m_new, l, acc

    init = (jnp.full((2 * tq, 1), -jnp.inf, F32), jnp.zeros((2 * tq, 1), F32),
            jnp.zeros((2 * tq, LANES), F32))
    carry = lax.fori_loop(0, i, lambda j, c: step(j, c, False), init)
    _, l, acc = step(i, carry, True)

    o = acc / l
    lam = (jnp.exp(jnp.sum(lq1_ref[...] * lk1_ref[...], axis=-1, keepdims=True))
           - jnp.exp(jnp.sum(lq2_ref[...] * lk2_ref[...], axis=-1, keepdims=True)) + LAM_INIT)
    d = o[:tq] - lam * o[tq:]
    o_ref[...] = ((_rms(d, SUBLN_EPS) * sw_ref[...]) * (1.0 - LAM_INIT)).astype(BF16)


def _attn(q, k, v, lq1, lk1, lq2, lk2, subln_w, bsz, seq):
    tokens = q.shape[0]
    nq = seq // ATT_BLOCK
    hw = 2 * DA_HEAD_DIM
    lam_spec = _resident((1, DA_HEAD_DIM))
    return pl.pallas_call(
        _attn_kernel,
        grid=(bsz, DA_HEADS, nq),
        in_specs=[
            lam_spec, lam_spec, lam_spec, lam_spec, _resident((1, hw)),
            pl.BlockSpec((ATT_BLOCK, hw), lambda b, h, i: (b * nq + i, h)),
            pl.BlockSpec((seq, hw), lambda b, h, i: (b, h)),
            pl.BlockSpec((seq, hw), lambda b, h, i: (b, h)),
        ],
        out_specs=pl.BlockSpec((ATT_BLOCK, hw), lambda b, h, i: (b * nq + i, h)),
        out_shape=jax.ShapeDtypeStruct((tokens, DA_WIDTH), BF16),
        compiler_params=pltpu.CompilerParams(
            dimension_semantics=("arbitrary", "arbitrary", "arbitrary"),
            vmem_limit_bytes=VMEM_LIMIT_BYTES),
        name="attn",
    )(lq1, lk1, lq2, lk2, subln_w, q, k, v)


def _merge_kernel(x_ref, ys_ref, ya_ref, g_ref, gate_ref, nw_ref, wso_ref, wao_ref, wout_ref, o_ref):
    d = x_ref.shape[1]
    so = jnp.dot(ys_ref[...], wso_ref[...], preferred_element_type=F32)
    ao = jnp.dot(ya_ref[...], wao_ref[...], preferred_element_type=F32)
    merged = g_ref[:, :d].astype(F32) * so + g_ref[:, d:].astype(F32) * ao
    mix = jnp.dot(merged.astype(BF16), wout_ref[...], preferred_element_type=F32)
    o_ref[...] = x_ref[...] + gate_ref[...] * (_rms(mix, NORM_EPS) * nw_ref[...])


def _merge(x2, y_ssd, y_att, gates, mod, post_norm_w, w_ssd_o, w_attn_o, w_out, seq):
    tokens, d = x2.shape
    tiles_per_seq = seq // ROW_TILE

    def row_spec(width):
        return pl.BlockSpec((ROW_TILE, width), lambda i: (i, 0))

    return pl.pallas_call(
        _merge_kernel,
        grid=(tokens // ROW_TILE,),
        in_specs=[
            row_spec(d), row_spec(d), row_spec(d), row_spec(2 * d),
            pl.BlockSpec((None, None, 1, d), lambda i: (2, i // tiles_per_seq, 0, 0)),
            _resident((1, d)), _resident((d, d)), _resident((d, d)), _resident((d, d)),
        ],
        out_specs=row_spec(d),
        out_shape=jax.ShapeDtypeStruct((tokens, d), F32),
        compiler_params=pltpu.CompilerParams(dimension_semantics=("arbitrary",),
                                             vmem_limit_bytes=VMEM_LIMIT_BYTES),
        name="merge",
    )(x2, y_ssd, y_att, gates, mod, post_norm_w, w_ssd_o, w_attn_o, w_out)


def _ffn_kernel(tiles_per_seq, x_ref, sc_ref, sh_ref, gate_ref, nw_ref, pw_ref, wup_ref, cw_ref,
                cb_ref, wdn_ref, o_ref, carry_ref, acc_ref):
    rows = x_ref.shape[0]
    first = (pl.program_id(0) % tiles_per_seq) == 0
    x = x_ref[...]
    h = (_rms(x, NORM_EPS) * nw_ref[...]) * (1.0 + sc_ref[...]) + sh_ref[...]
    hb = h.astype(BF16)
    k0 = math.sqrt(2.0 / math.pi)

    def conv_cols(c0):
        u = jnp.dot(hb, wup_ref[:, c0:c0 + FF_CHUNK], preferred_element_type=F32)
        prev8 = jnp.where(first, 0.0, carry_ref[:, c0:c0 + FF_CHUNK])
        y = _causal_dwconv_tile(u, prev8, cw_ref[:, c0:c0 + FF_CHUNK], cb_ref[:, c0:c0 + FF_CHUNK])
        carry_ref[:, c0:c0 + FF_CHUNK] = u[rows - SUBLANES:]
        return y

    for idx, c in enumerate(range(0, D_FF, FF_CHUNK)):
        ug = conv_cols(c)
        uv = conv_cols(D_FF + c)
        act = 0.5 * ug * (1.0 + jnp.tanh(k0 * (ug + 0.044715 * (ug * ug * ug)))) * uv
        part = jnp.dot(act.astype(BF16), wdn_ref[c:c + FF_CHUNK, :], preferred_element_type=F32)
        if idx == 0:
            acc_ref[...] = part
        else:
            acc_ref[...] += part

    o_ref[...] = x + gate_ref[...] * (_rms(acc_ref[...], NORM_EPS) * pw_ref[...])


def _ffn(x1, mod, pre_norm_w, post_norm_w, w_up, conv_w, conv_b, w_down, seq):
    tokens, d = x1.shape
    tiles_per_seq = seq // ROW_TILE

    def row_spec(width):
        return pl.BlockSpec((ROW_TILE, width), lambda i: (i, 0))

    def mod_spec(which):
        return pl.BlockSpec((None, None, 1, d), lambda i: (which, i // tiles_per_seq, 0, 0))

    return pl.pallas_call(
        functools.partial(_ffn_kernel, tiles_per_seq),
        grid=(tokens // ROW_TILE,),
        in_specs=[
            row_spec(d), mod_spec(4), mod_spec(3), mod_spec(5), _resident((1, d)), _resident((1, d)),
            _resident((d, 2 * D_FF)), _resident((FFN_CONV, 2 * D_FF)), _resident((1, 2 * D_FF)),
            _resident((D_FF, d)),
        ],
        out_specs=row_spec(d),
        out_shape=jax.ShapeDtypeStruct((tokens, d), F32),
        scratch_shapes=[pltpu.VMEM((SUBLANES, 2 * D_FF), F32), pltpu.VMEM((ROW_TILE, d), F32)],
        compiler_params=pltpu.CompilerParams(dimension_semantics=("arbitrary",),
                                             vmem_limit_bytes=VMEM_LIMIT_BYTES),
        name="ffn",
    )(x1, mod, mod, mod, pre_norm_w, post_norm_w, w_up, conv_w, conv_b, w_down)


def kernel(x, c, w_ada, b_ada, pre_norm1_w, w_in, conv_ssd_w, conv_ssd_b, dt_bias, a_log, d_skip,
           ssd_norm_w, w_ssd_o, lambda_q1, lambda_k1, lambda_q2, lambda_k2, subln_w, w_attn_o, w_out,
           post_norm1_w, pre_norm2_w, w_up, conv_ffn_w, conv_ffn_b, w_down, post_norm2_w):
    bsz, seq, d = x.shape
    assert d == D_MODEL and seq % ROW_TILE == 0 and seq % ATT_BLOCK == 0 and seq % SSD_CHUNK == 0
    assert w_ada.shape[0] == 1, "single-layer block"
    layer = 0
    x2 = x.reshape(bsz * seq, d)

    mod = _ada(c, w_ada[layer], b_ada[layer]).reshape(6, bsz, 1, d)

    w = w_in[layer]
    s0 = SSD_D_INNER
    s1 = s0 + SSD_CONV_DIM
    s2 = s1 + SSD_HEADS
    s3 = s2 + DA_WIDTH
    s4 = s3 + DA_WIDTH
    s5 = s4 + DA_WIDTH
    w_cat = jnp.concatenate(
        [w[:, :s0], w[:, s0:s1], w[:, s2:s3], w[:, s3:s4], w[:, s4:s5], w[:, s5:],
         jnp.pad(w[:, s1:s2], ((0, 0), (0, LANES - SSD_HEADS)))], axis=1).astype(BF16)
    dt_bias_pad = jnp.pad(dt_bias[layer], (0, LANES - SSD_HEADS)).reshape(1, LANES)

    z, xbc, q, k, v, gates, dt = _inproj(
        x2, mod, pre_norm1_w[layer].reshape(1, d), w_cat, conv_ssd_w[layer],
        conv_ssd_b[layer].reshape(1, -1), dt_bias_pad, seq)

    a_row = jnp.pad(-jnp.exp(a_log[layer].astype(F32)), (0, LANES - SSD_HEADS)).reshape(1, LANES)
    dskip_x = jnp.repeat(d_skip[layer].astype(F32), SSD_HEAD_DIM).reshape(1, SSD_D_INNER)
    y_ssd = _ssd(xbc, dt, z, a_row, dskip_x, ssd_norm_w[layer].reshape(1, -1), bsz, seq)

    y_att = _attn(q, k, v, lambda_q1[layer].reshape(1, -1), lambda_k1[layer].reshape(1, -1),
                  lambda_q2[layer].reshape(1, -1), lambda_k2[layer].reshape(1, -1),
                  subln_w[layer].reshape(1, -1), bsz, seq)

    x1 = _merge(x2, y_ssd, y_att, gates, mod, post_norm1_w[layer].reshape(1, d),
                w_ssd_o[layer].astype(BF16), w_attn_o[layer].astype(BF16), w_out[layer].astype(BF16),
                seq)

    out = _ffn(x1, mod, pre_norm2_w[layer].reshape(1, d), post_norm2_w[layer].reshape(1, d),
               w_up[layer].astype(BF16), conv_ffn_w[layer], conv_ffn_b[layer].reshape(1, -1),
               w_down[layer].astype(BF16), seq)
    return out.reshape(bsz, seq, d)
```

```python
import functools
import math

import jax
import jax.numpy as jnp
from jax import lax
from jax.experimental import pallas as pl
from jax.experimental.pallas import tpu as pltpu

F32 = jnp.float32
BF16 = jnp.bfloat16

D_MODEL = 1024
SSD_D_INNER = 1024
SSD_HEAD_DIM = 64
SSD_HEADS = SSD_D_INNER // SSD_HEAD_DIM
SSD_GROUPS = 2
SSD_D_STATE = 128
SSD_CONV = 4
SSD_CHUNK = 128
SSD_CONV_DIM = SSD_D_INNER + 2 * SSD_GROUPS * SSD_D_STATE
SSD_NORM_EPS = 1e-5
DA_HEADS = 8
DA_HEAD_DIM = 64
DA_WIDTH = DA_HEADS * 2 * DA_HEAD_DIM
ROPE_THETA = 500000.0
ROPE_DIM = DA_HEAD_DIM // 4
SUBLN_EPS = 1e-5
D_FF = 2816
FFN_CONV = 3
NORM_EPS = 1e-6
LAM_INIT = 0.8 - 0.6 * math.exp(-0.3 * 0)

LANES = 128
SUBLANES = 8
VMEM_LIMIT_BYTES = 56 * 1024 * 1024

ROW_TILE = 512
COL_CHUNK = 512
FF_CHUNK = 256
ATT_BLOCK = 256


def _silu(x):
    return x * jax.nn.sigmoid(x)


def _rms(x, eps):
    return x * lax.rsqrt(jnp.mean(x * x, axis=-1, keepdims=True) + eps)


def _resident(shape):
    nd = len(shape)
    return pl.BlockSpec(shape, lambda *_: (0,) * nd, pipeline_mode=pl.Buffered(1))


def _causal_dwconv_tile(u, prev8, w, bias):
    taps = w.shape[0]
    y = u * w[taps - 1:taps] + bias
    for k in range(1, taps):
        y = y + pltpu.roll(u, k, axis=0) * w[taps - 1 - k:taps - k]
    top_rows = 2 * SUBLANES
    top = u[:top_rows]
    prev = jnp.concatenate([jnp.zeros_like(prev8), prev8], axis=0)
    row = lax.broadcasted_iota(jnp.int32, top.shape, 0)
    ytop = top * w[taps - 1:taps] + bias
    for k in range(1, taps):
        shifted = jnp.where(row < k, pltpu.roll(prev, k, axis=0), pltpu.roll(top, k, axis=0))
        ytop = ytop + shifted * w[taps - 1 - k:taps - k]
    return jnp.concatenate([ytop, y[top_rows:]], axis=0)


def _ada_kernel(c_ref, w_ref, b_ref, o_ref):
    c = c_ref[...]
    o_ref[0] = jnp.dot(_silu(c), w_ref[...], preferred_element_type=F32,
                       precision=lax.Precision.HIGHEST) + b_ref[0]


def _ada(c, w_ada, b_ada):
    bsz, d = c.shape
    n_mod = w_ada.shape[1] // d
    return pl.pallas_call(
        _ada_kernel,
        grid=(n_mod,),
        in_specs=[
            pl.BlockSpec((bsz, d), lambda j: (0, 0)),
            pl.BlockSpec((d, d), lambda j: (0, j)),
            pl.BlockSpec((1, 1, d), lambda j: (j, 0, 0)),
        ],
        out_specs=pl.BlockSpec((1, bsz, d), lambda j: (j, 0, 0)),
        out_shape=jax.ShapeDtypeStruct((n_mod, bsz, d), F32),
        compiler_params=pltpu.CompilerParams(dimension_semantics=("arbitrary",),
                                             vmem_limit_bytes=VMEM_LIMIT_BYTES),
        name="ada",
    )(c, w_ada, b_ada.reshape(n_mod, 1, d))


_OFF_Z = 0
_OFF_XBC = _OFF_Z + SSD_D_INNER
_OFF_Q = _OFF_XBC + SSD_CONV_DIM
_OFF_K = _OFF_Q + DA_WIDTH
_OFF_V = _OFF_K + DA_WIDTH
_OFF_G = _OFF_V + DA_WIDTH
_OFF_DT = _OFF_G + 2 * D_MODEL
_W_COLS = _OFF_DT + LANES


def _inproj_kernel(tiles_per_seq, x_ref, sc_ref, sh_ref, nw_ref, w_ref, cosf_ref, sina_ref,
                   sinb_ref, cw_ref, cb_ref, dtb_ref,
                   z_ref, xbc_ref, qt_ref, k_ref, vt_ref, g_ref, dt_ref, carry_ref):
    rows = x_ref.shape[0]
    first = (pl.program_id(0) % tiles_per_seq) == 0
    x = x_ref[...]
    h = (_rms(x, NORM_EPS) * nw_ref[...]) * (1.0 + sc_ref[...]) + sh_ref[...]
    hb = h.astype(BF16)

    def mm(c0, width):
        return jnp.dot(hb, w_ref[:, c0:c0 + width], preferred_element_type=F32)

    for c in range(0, SSD_D_INNER, COL_CHUNK):
        z_ref[:, c:c + COL_CHUNK] = mm(_OFF_Z + c, COL_CHUNK).astype(BF16)

    for c in range(0, SSD_CONV_DIM, COL_CHUNK):
        u = mm(_OFF_XBC + c, COL_CHUNK)
        prev8 = jnp.where(first, 0.0, carry_ref[:, c:c + COL_CHUNK])
        y = _causal_dwconv_tile(u, prev8, cw_ref[:, c:c + COL_CHUNK], cb_ref[:, c:c + COL_CHUNK])
        carry_ref[:, c:c + COL_CHUNK] = u[rows - SUBLANES:]
        xbc_ref[:, c:c + COL_CHUNK] = _silu(y).astype(BF16)

    cosf = cosf_ref[...]
    sina = sina_ref[...]
    sinb = sinb_ref[...]
    half = ROPE_DIM // 2
    def rope(t):
        return (t * cosf + pltpu.roll(t, LANES - half, axis=1) * sina
                + pltpu.roll(t, half, axis=1) * sinb)

    def store_transposed(out_ref, c, val):
        out_ref[c:c + COL_CHUNK, :] = val.T.astype(BF16)

    q_scale = DA_HEAD_DIM ** -0.5
    for c in range(0, DA_WIDTH, COL_CHUNK):
        acc = mm(_OFF_Q + c, COL_CHUNK)
        r = jnp.concatenate([rope(acc[:, s:s + LANES]) * q_scale
                             for s in range(0, COL_CHUNK, LANES)], axis=1)
        store_transposed(qt_ref, c, r)

    for c in range(0, DA_WIDTH, COL_CHUNK):
        acc = mm(_OFF_K + c, COL_CHUNK)
        for s in range(0, COL_CHUNK, LANES):
            k_ref[:, c + s:c + s + LANES] = rope(acc[:, s:s + LANES]).astype(BF16)

    for c in range(0, DA_WIDTH, COL_CHUNK):
        store_transposed(vt_ref, c, mm(_OFF_V + c, COL_CHUNK))

    for c in range(0, 2 * D_MODEL, COL_CHUNK):
        g_ref[:, c:c + COL_CHUNK] = jax.nn.sigmoid(mm(_OFF_G + c, COL_CHUNK)).astype(BF16)

    dt_raw = mm(_OFF_DT, LANES) + dtb_ref[...]
    dt_ref[...] = jnp.maximum(dt_raw, 0.0) + jnp.log1p(jnp.exp(-jnp.abs(dt_raw)))


def _rope_tables(seq):
    pos = jnp.arange(seq, dtype=F32)
    inv_freq = jnp.power(ROPE_THETA, -jnp.arange(0, ROPE_DIM, 2, dtype=F32) / ROPE_DIM)
    ang = pos[:, None] * inv_freq[None, :]
    cos, sin = jnp.cos(ang), jnp.sin(ang)
    half = ROPE_DIM // 2
    d = jnp.arange(LANES) % DA_HEAD_DIM
    idx = d % half
    cosf = jnp.where(d[None, :] < ROPE_DIM, cos[:, idx], 1.0)
    sina = jnp.where(d[None, :] < half, -sin[:, idx], 0.0)
    sinb = jnp.where((d[None, :] >= half) & (d[None, :] < ROPE_DIM), sin[:, idx], 0.0)
    return cosf.astype(F32), sina.astype(F32), sinb.astype(F32)


def _inproj(x2, mod, pre_norm_w, w_cat, conv_w, conv_b, dt_bias_pad, seq):
    tokens, d = x2.shape
    tiles_per_seq = seq // ROW_TILE
    cosf, sina, sinb = _rope_tables(seq)

    def row_spec(width):
        return pl.BlockSpec((ROW_TILE, width), lambda i: (i, 0))

    def mod_spec(which):
        return pl.BlockSpec((None, None, 1, d), lambda i: (which, i // tiles_per_seq, 0, 0))

    def rope_spec():
        return pl.BlockSpec((ROW_TILE, LANES), lambda i: (i % tiles_per_seq, 0))

    def transposed_spec():
        return pl.BlockSpec((DA_WIDTH, ROW_TILE), lambda i: (0, i))

    transposed = jax.ShapeDtypeStruct((DA_WIDTH, tokens), BF16)
    out_shapes = (
        jax.ShapeDtypeStruct((tokens, SSD_D_INNER), BF16),
        jax.ShapeDtypeStruct((tokens, SSD_CONV_DIM), BF16),
        transposed,
        jax.ShapeDtypeStruct((tokens, DA_WIDTH), BF16),
        transposed,
        jax.ShapeDtypeStruct((tokens, 2 * D_MODEL), BF16),
        jax.ShapeDtypeStruct((tokens, LANES), F32),
    )
    return pl.pallas_call(
        functools.partial(_inproj_kernel, tiles_per_seq),
        grid=(tokens // ROW_TILE,),
        in_specs=[
            row_spec(d), mod_spec(1), mod_spec(0), _resident((1, d)), _resident((d, _W_COLS)),
            rope_spec(), rope_spec(), rope_spec(),
            _resident((SSD_CONV, SSD_CONV_DIM)), _resident((1, SSD_CONV_DIM)), _resident((1, LANES)),
        ],
        out_specs=[transposed_spec() if s is transposed else row_spec(s.shape[1])
                   for s in out_shapes],
        out_shape=out_shapes,
        scratch_shapes=[pltpu.VMEM((SUBLANES, SSD_CONV_DIM), F32)],
        compiler_params=pltpu.CompilerParams(dimension_semantics=("arbitrary",),
                                             vmem_limit_bytes=VMEM_LIMIT_BYTES),
        name="inproj",
    )(x2, mod, mod, pre_norm_w, w_cat, cosf, sina, sinb, conv_w, conv_b, dt_bias_pad)


def _split3(a):
    hi = a.astype(BF16)
    r = a - hi.astype(F32)
    mid = r.astype(BF16)
    lo = (r - mid.astype(F32)).astype(BF16)
    return hi, mid, lo


def _ssd_kernel(xbc_ref, dt_ref, z_ref, a_ref, dskip_ref, nw_ref, expand_ref, o_ref, state_ref):
    l = SSD_CHUNK
    n = SSD_D_STATE
    gw = SSD_D_INNER // SSD_GROUPS

    @pl.when(pl.program_id(1) == 0)
    def _():
        state_ref[...] = jnp.zeros_like(state_ref)

    xbc = xbc_ref[...]
    xs = xbc[:, :SSD_D_INNER].astype(F32)
    dt = dt_ref[...]
    is_head = lax.broadcasted_iota(jnp.int32, (1, LANES), 1) < SSD_HEADS
    la = dt * jnp.where(is_head, -jnp.exp(a_ref[...]), 0.0)
    row = lax.broadcasted_iota(jnp.int32, (l, l), 0)
    col = lax.broadcasted_iota(jnp.int32, (l, l), 1)
    causal = row >= col
    tril = jnp.where(causal, 1.0, 0.0).astype(BF16)
    expand = expand_ref[...]

    acum = sum(jnp.dot(tril, part, preferred_element_type=F32) for part in _split3(la))
    acum_t = acum.T
    alast = acum[l - 1:l, :]
    dte = jnp.exp(alast - acum)
    dt_x = jnp.dot(dt.astype(BF16), expand, preferred_element_type=F32)
    w_x = jnp.dot((dt * dte).astype(BF16), expand, preferred_element_type=F32)
    cdec = jnp.broadcast_to(jnp.exp(alast), (SUBLANES, LANES))
    cdec_x = sum(jnp.dot(part, expand, preferred_element_type=F32) for part in _split3(cdec))[0:1]

    xd = (xs * dt_x).astype(BF16)
    xw = (xs * w_x).astype(BF16)
    lane = lax.broadcasted_iota(jnp.int32, (l, LANES), 1)

    pairs = []
    for g in range(SSD_GROUPS):
        bm = xbc[:, SSD_D_INNER + g * n:SSD_D_INNER + (g + 1) * n]
        cm = xbc[:, SSD_D_INNER + (SSD_GROUPS + g) * n:SSD_D_INNER + (SSD_GROUPS + g + 1) * n]
        cb = lax.dot_general(cm, bm, (((1,), (1,)), ((), ())), preferred_element_type=F32)
        cmf = cm.astype(F32)
        state = state_ref[g]
        state_b = state.astype(BF16)
        heads_per_group = SSD_HEADS // SSD_GROUPS
        for pr in range(heads_per_group // 2):
            res = []
            for sub in range(2):
                h = g * heads_per_group + 2 * pr + sub
                a_col = acum[:, h:h + 1]
                seg = a_col - acum_t[h:h + 1, :]
                decay = jnp.exp(jnp.where(causal, seg, -jnp.inf))
                lhs = jnp.concatenate([(cb * decay).astype(BF16),
                                       (cmf * jnp.exp(a_col)).astype(BF16)], axis=1)
                c0 = g * gw + pr * LANES
                rhs = jnp.concatenate([xd[:, c0:c0 + LANES],
                                       state_b[:, pr * LANES:(pr + 1) * LANES]], axis=0)
                res.append(jnp.dot(lhs, rhs, preferred_element_type=F32))
            pairs.append(jnp.where(lane < SSD_HEAD_DIM, res[0], res[1]))
        bm_t = bm.astype(F32).T.astype(BF16)
        new = jnp.dot(bm_t, xw[:, g * gw:(g + 1) * gw], preferred_element_type=F32)
        state_ref[g] = state * cdec_x[:, g * gw:(g + 1) * gw] + new

    y = jnp.concatenate(pairs, axis=1)
    y = y + dskip_ref[...] * xs
    y = y * _silu(z_ref[...].astype(F32))
    y = jnp.concatenate([_rms(y[:, g * gw:(g + 1) * gw], SSD_NORM_EPS) for g in range(SSD_GROUPS)],
                        axis=1)
    o_ref[...] = (y * nw_ref[...]).astype(BF16)


def _ssd(xbc, dt, z, a_row, dskip_x, norm_w, bsz, seq):
    tokens = xbc.shape[0]
    nc = seq // SSD_CHUNK
    head = jnp.arange(LANES)[:, None]
    chan = jnp.arange(SSD_D_INNER)[None, :] // SSD_HEAD_DIM
    expand = (head == chan).astype(BF16)

    def row_spec(width):
        return pl.BlockSpec((SSD_CHUNK, width), lambda b, c: (b * nc + c, 0))

    return pl.pallas_call(
        _ssd_kernel,
        grid=(bsz, nc),
        in_specs=[
            row_spec(SSD_CONV_DIM), row_spec(LANES), row_spec(SSD_D_INNER),
            _resident((1, LANES)), _resident((1, SSD_D_INNER)), _resident((1, SSD_D_INNER)),
            _resident((LANES, SSD_D_INNER)),
        ],
        out_specs=row_spec(SSD_D_INNER),
        out_shape=jax.ShapeDtypeStruct((tokens, SSD_D_INNER), BF16),
        scratch_shapes=[pltpu.VMEM((SSD_GROUPS, SSD_D_STATE, SSD_D_INNER // SSD_GROUPS), F32)],
        compiler_params=pltpu.CompilerParams(dimension_semantics=("arbitrary", "arbitrary"),
                                             vmem_limit_bytes=VMEM_LIMIT_BYTES),
        name="ssd",
    )(xbc, dt, z, a_row, dskip_x, norm_w, expand)


def _attn_kernel(lq1_ref, lk1_ref, lq2_ref, lk2_ref, sw_ref, qt_ref, k_ref, vt_ref, o_ref,
                 st_ref, p_ref):
    hw, seq = qt_ref.shape
    t = ATT_BLOCK
    lam = (jnp.exp(jnp.sum(lq1_ref[...] * lk1_ref[...], axis=-1, keepdims=True))
           - jnp.exp(jnp.sum(lq2_ref[...] * lk2_ref[...], axis=-1, keepdims=True)) + LAM_INIT)
    dim = lax.broadcasted_iota(jnp.int32, (hw, t), 0)
    query = lax.broadcasted_iota(jnp.int32, (t, 2 * t), 1)
    query = jnp.where(query >= t, query - t, query)
    key_le_query = lax.broadcasted_iota(jnp.int32, (t, 2 * t), 0) <= query

    for i in range(seq // t):
        qt = qt_ref[:, i * t:(i + 1) * t]
        zero = jnp.zeros_like(qt)
        qm = jnp.concatenate([jnp.where(dim < DA_HEAD_DIM, qt, zero),
                              jnp.where(dim >= DA_HEAD_DIM, qt, zero)], axis=1)
        m = jnp.full((1, 2 * t), -jnp.inf, F32)
        for j in range(i + 1):
            st = jnp.dot(k_ref[j * t:(j + 1) * t, :], qm, preferred_element_type=F32)
            if j == i:
                st = jnp.where(key_le_query, st, -jnp.inf)
            st_ref[j * t:(j + 1) * t, :] = st
            m = jnp.maximum(m, jnp.max(st, axis=0, keepdims=True))
        l = jnp.zeros((1, 2 * t), F32)
        for j in range(i + 1):
            e = jnp.exp(st_ref[j * t:(j + 1) * t, :] - m)
            l = l + jnp.sum(e, axis=0, keepdims=True)
            p_ref[j * t:(j + 1) * t, :] = e.astype(BF16)
        n_keys = (i + 1) * t
        acc = jnp.dot(vt_ref[:, :n_keys], p_ref[:n_keys, :], preferred_element_type=F32) / l
        d = acc[:, :t] - lam * acc[:, t:]
        d = d * lax.rsqrt(jnp.mean(d * d, axis=0, keepdims=True) + SUBLN_EPS)
        y = (d.T * sw_ref[...]) * (1.0 - LAM_INIT)
        o_ref[i * t:(i + 1) * t, :] = y.astype(BF16)


def _attn(qt, k, vt, lq1, lk1, lq2, lk2, subln_w, bsz, seq):
    tokens = k.shape[0]
    hw = 2 * DA_HEAD_DIM
    lam_spec = _resident((1, DA_HEAD_DIM))
    transposed_spec = pl.BlockSpec((hw, seq), lambda b, h: (h, b))
    return pl.pallas_call(
        _attn_kernel,
        grid=(bsz, DA_HEADS),
        in_specs=[
            lam_spec, lam_spec, lam_spec, lam_spec, _resident((1, hw)),
            transposed_spec,
            pl.BlockSpec((seq, hw), lambda b, h: (b, h)),
            transposed_spec,
        ],
        out_specs=pl.BlockSpec((seq, hw), lambda b, h: (b, h)),
        out_shape=jax.ShapeDtypeStruct((tokens, DA_WIDTH), BF16),
        scratch_shapes=[pltpu.VMEM((seq, 2 * ATT_BLOCK), F32), pltpu.VMEM((seq, 2 * ATT_BLOCK), BF16)],
        compiler_params=pltpu.CompilerParams(dimension_semantics=("arbitrary", "arbitrary"),
                                             vmem_limit_bytes=VMEM_LIMIT_BYTES),
        name="attn",
    )(lq1, lk1, lq2, lk2, subln_w, qt, k, vt)


def _merge_kernel(x_ref, ys_ref, ya_ref, g_ref, gate_ref, nw_ref, wso_ref, wao_ref, wout_ref, o_ref):
    d = x_ref.shape[1]
    so = jnp.dot(ys_ref[...], wso_ref[...], preferred_element_type=F32)
    ao = jnp.dot(ya_ref[...], wao_ref[...], preferred_element_type=F32)
    merged = g_ref[:, :d].astype(F32) * so + g_ref[:, d:].astype(F32) * ao
    mix = jnp.dot(merged.astype(BF16), wout_ref[...], preferred_element_type=F32)
    o_ref[...] = x_ref[...] + gate_ref[...] * (_rms(mix, NORM_EPS) * nw_ref[...])


def _merge(x2, y_ssd, y_att, gates, mod, post_norm_w, w_ssd_o, w_attn_o, w_out, seq):
    tokens, d = x2.shape
    tiles_per_seq = seq // ROW_TILE

    def row_spec(width):
        return pl.BlockSpec((ROW_TILE, width), lambda i: (i, 0))

    return pl.pallas_call(
        _merge_kernel,
        grid=(tokens // ROW_TILE,),
        in_specs=[
            row_spec(d), row_spec(d), row_spec(d), row_spec(2 * d),
            pl.BlockSpec((None, None, 1, d), lambda i: (2, i // tiles_per_seq, 0, 0)),
            _resident((1, d)), _resident((d, d)), _resident((d, d)), _resident((d, d)),
        ],
        out_specs=row_spec(d),
        out_shape=jax.ShapeDtypeStruct((tokens, d), F32),
        compiler_params=pltpu.CompilerParams(dimension_semantics=("arbitrary",),
                                             vmem_limit_bytes=VMEM_LIMIT_BYTES),
        name="merge",
    )(x2, y_ssd, y_att, gates, mod, post_norm_w, w_ssd_o, w_attn_o, w_out)


def _ffn_kernel(tiles_per_seq, x_ref, sc_ref, sh_ref, gate_ref, nw_ref, pw_ref, wup_ref, cw_ref,
                cb_ref, wdn_ref, o_ref, carry_ref, acc_ref):
    rows = x_ref.shape[0]
    first = (pl.program_id(0) % tiles_per_seq) == 0
    x = x_ref[...]
    h = (_rms(x, NORM_EPS) * nw_ref[...]) * (1.0 + sc_ref[...]) + sh_ref[...]
    hb = h.astype(BF16)
    k0 = math.sqrt(2.0 / math.pi)

    def conv_cols(c0):
        u = jnp.dot(hb, wup_ref[:, c0:c0 + FF_CHUNK], preferred_element_type=F32)
        prev8 = jnp.where(first, 0.0, carry_ref[:, c0:c0 + FF_CHUNK])
        y = _causal_dwconv_tile(u, prev8, cw_ref[:, c0:c0 + FF_CHUNK], cb_ref[:, c0:c0 + FF_CHUNK])
        carry_ref[:, c0:c0 + FF_CHUNK] = u[rows - SUBLANES:]
        return y

    for idx, c in enumerate(range(0, D_FF, FF_CHUNK)):
        ug = conv_cols(c)
        uv = conv_cols(D_FF + c)
        act = 0.5 * ug * (1.0 + jnp.tanh(k0 * (ug + 0.044715 * (ug * ug * ug)))) * uv
        part = jnp.dot(act.astype(BF16), wdn_ref[c:c + FF_CHUNK, :], preferred_element_type=F32)
        if idx == 0:
            acc_ref[...] = part
        else:
            acc_ref[...] += part

    o_ref[...] = x + gate_ref[...] * (_rms(acc_ref[...], NORM_EPS) * pw_ref[...])


def _ffn(x1, mod, pre_norm_w, post_norm_w, w_up, conv_w, conv_b, w_down, seq):
    tokens, d = x1.shape
    tiles_per_seq = seq // ROW_TILE

    def row_spec(width):
        return pl.BlockSpec((ROW_TILE, width), lambda i: (i, 0))

    def mod_spec(which):
        return pl.BlockSpec((None, None, 1, d), lambda i: (which, i // tiles_per_seq, 0, 0))

    return pl.pallas_call(
        functools.partial(_ffn_kernel, tiles_per_seq),
        grid=(tokens // ROW_TILE,),
        in_specs=[
            row_spec(d), mod_spec(4), mod_spec(3), mod_spec(5), _resident((1, d)), _resident((1, d)),
            _resident((d, 2 * D_FF)), _resident((FFN_CONV, 2 * D_FF)), _resident((1, 2 * D_FF)),
            _resident((D_FF, d)),
        ],
        out_specs=row_spec(d),
        out_shape=jax.ShapeDtypeStruct((tokens, d), F32),
        scratch_shapes=[pltpu.VMEM((SUBLANES, 2 * D_FF), F32), pltpu.VMEM((ROW_TILE, d), F32)],
        compiler_params=pltpu.CompilerParams(dimension_semantics=("arbitrary",),
                                             vmem_limit_bytes=VMEM_LIMIT_BYTES),
        name="ffn",
    )(x1, mod, mod, mod, pre_norm_w, post_norm_w, w_up, conv_w, conv_b, w_down)


def kernel(x, c, w_ada, b_ada, pre_norm1_w, w_in, conv_ssd_w, conv_ssd_b, dt_bias, a_log, d_skip,
           ssd_norm_w, w_ssd_o, lambda_q1, lambda_k1, lambda_q2, lambda_k2, subln_w, w_attn_o, w_out,
           post_norm1_w, pre_norm2_w, w_up, conv_ffn_w, conv_ffn_b, w_down, post_norm2_w):
    bsz, seq, d = x.shape
    assert d == D_MODEL and seq % ROW_TILE == 0 and seq % ATT_BLOCK == 0 and seq % SSD_CHUNK == 0
    assert w_ada.shape[0] == 1, "single-layer block"
    layer = 0
    x2 = x.reshape(bsz * seq, d)

    mod = _ada(c, w_ada[layer], b_ada[layer]).reshape(6, bsz, 1, d)

    w = w_in[layer]
    s0 = SSD_D_INNER
    s1 = s0 + SSD_CONV_DIM
    s2 = s1 + SSD_HEADS
    s3 = s2 + DA_WIDTH
    s4 = s3 + DA_WIDTH
    s5 = s4 + DA_WIDTH
    w_cat = jnp.concatenate(
        [w[:, :s0], w[:, s0:s1], w[:, s2:s3], w[:, s3:s4], w[:, s4:s5], w[:, s5:],
         jnp.pad(w[:, s1:s2], ((0, 0), (0, LANES - SSD_HEADS)))], axis=1).astype(BF16)
    dt_bias_pad = jnp.pad(dt_bias[layer], (0, LANES - SSD_HEADS)).reshape(1, LANES)

    z, xbc, qt, k, vt, gates, dt = _inproj(
        x2, mod, pre_norm1_w[layer].reshape(1, d), w_cat, conv_ssd_w[layer],
        conv_ssd_b[layer].reshape(1, -1), dt_bias_pad, seq)

    a_row = jnp.pad(a_log[layer].astype(F32), (0, LANES - SSD_HEADS)).reshape(1, LANES)
    dskip_x = jnp.repeat(d_skip[layer].astype(F32), SSD_HEAD_DIM).reshape(1, SSD_D_INNER)
    y_ssd = _ssd(xbc, dt, z, a_row, dskip_x, ssd_norm_w[layer].reshape(1, -1), bsz, seq)

    y_att = _attn(qt, k, vt, lambda_q1[layer].reshape(1, -1), lambda_k1[layer].reshape(1, -1),
                  lambda_q2[layer].reshape(1, -1), lambda_k2[layer].reshape(1, -1),
                  subln_w[layer].reshape(1, -1), bsz, seq)

    x1 = _merge(x2, y_ssd, y_att, gates, mod, post_norm1_w[layer].reshape(1, d),
                w_ssd_o[layer].astype(BF16), w_attn_o[layer].astype(BF16), w_out[layer].astype(BF16),
                seq)

    out = _ffn(x1, mod, pre_norm2_w[layer].reshape(1, d), post_norm2_w[layer].reshape(1, d),
               w_up[layer].astype(BF16), conv_ffn_w[layer], conv_ffn_b[layer].reshape(1, -1),
               w_down[layer].astype(BF16), seq)
    return out.reshape(bsz, seq, d)
```

```python
import functools
import math

import jax
import jax.numpy as jnp
from jax import lax
from jax.experimental import pallas as pl
from jax.experimental.pallas import tpu as pltpu

F32 = jnp.float32
BF16 = jnp.bfloat16

D_MODEL = 1024
SSD_D_INNER = 1024
SSD_HEAD_DIM = 64
SSD_HEADS = SSD_D_INNER // SSD_HEAD_DIM
SSD_GROUPS = 2
SSD_D_STATE = 128
SSD_CONV = 4
SSD_CHUNK = 128
SSD_CONV_DIM = SSD_D_INNER + 2 * SSD_GROUPS * SSD_D_STATE
SSD_NORM_EPS = 1e-5
DA_HEADS = 8
DA_HEAD_DIM = 64
DA_WIDTH = DA_HEADS * 2 * DA_HEAD_DIM
ROPE_THETA = 500000.0
ROPE_DIM = DA_HEAD_DIM // 4
SUBLN_EPS = 1e-5
D_FF = 2816
FFN_CONV = 3
NORM_EPS = 1e-6
LAM_INIT = 0.8 - 0.6 * math.exp(-0.3 * 0)

LANES = 128
SUBLANES = 8
VMEM_LIMIT_BYTES = 56 * 1024 * 1024

ROW_TILE = 512
COL_CHUNK = 512
FF_CHUNK = 256
ATT_BLOCK = 256
ROW_BLOCK = 128


def _silu(x):
    return x * jax.nn.sigmoid(x)


def _rms(x, eps):
    return x * lax.rsqrt(jnp.mean(x * x, axis=-1, keepdims=True) + eps)


def _resident(shape):
    nd = len(shape)
    return pl.BlockSpec(shape, lambda *_: (0,) * nd, pipeline_mode=pl.Buffered(1))


def _ada_kernel(c_ref, w_ref, b_ref, o_ref):
    c = c_ref[...]
    o_ref[0] = jnp.dot(_silu(c), w_ref[...], preferred_element_type=F32,
                       precision=lax.Precision.HIGHEST) + b_ref[0]


def _ada(c, w_ada, b_ada):
    bsz, d = c.shape
    n_mod = w_ada.shape[1] // d
    return pl.pallas_call(
        _ada_kernel,
        grid=(n_mod,),
        in_specs=[
            pl.BlockSpec((bsz, d), lambda j: (0, 0)),
            pl.BlockSpec((d, d), lambda j: (0, j)),
            pl.BlockSpec((1, 1, d), lambda j: (j, 0, 0)),
        ],
        out_specs=pl.BlockSpec((1, bsz, d), lambda j: (j, 0, 0)),
        out_shape=jax.ShapeDtypeStruct((n_mod, bsz, d), F32),
        compiler_params=pltpu.CompilerParams(dimension_semantics=("arbitrary",),
                                             vmem_limit_bytes=VMEM_LIMIT_BYTES),
        name="ada",
    )(c, w_ada, b_ada.reshape(n_mod, 1, d))


_OFF_Z = 0
_OFF_XBC = _OFF_Z + SSD_D_INNER
_OFF_Q = _OFF_XBC + SSD_CONV_DIM
_OFF_K = _OFF_Q + DA_WIDTH
_OFF_V = _OFF_K + DA_WIDTH
_OFF_G = _OFF_V + DA_WIDTH
_OFF_DT = _OFF_G + 2 * D_MODEL
_W_COLS = _OFF_DT + LANES


def _inproj_kernel(tiles_per_seq, x_ref, sc_ref, sh_ref, nw_ref, w_ref, cosf_ref, sina_ref,
                   sinb_ref, cw_ref, cb_ref, dtb_ref,
                   z_ref, xbc_ref, qt_ref, k_ref, vt_ref, g_ref, dt_ref, carry_ref, hb_ref, u_ref):
    rows = x_ref.shape[0]
    first = (pl.program_id(0) % tiles_per_seq) == 0
    x = x_ref[...]
    h = (_rms(x, NORM_EPS) * nw_ref[...]) * (1.0 + sc_ref[...]) + sh_ref[...]
    hb_ref[...] = h.astype(BF16)
    half = ROPE_DIM // 2
    q_scale = DA_HEAD_DIM ** -0.5 * math.log2(math.e)

    def rope(t, r0):
        rs = slice(r0, r0 + ROW_BLOCK)
        return (t * cosf_ref[rs, :] + pltpu.roll(t, LANES - half, axis=1) * sina_ref[rs, :]
                + pltpu.roll(t, half, axis=1) * sinb_ref[rs, :])

    def block(slot, r0, width=COL_CHUNK):
        return u_ref[slot, SUBLANES + r0:SUBLANES + r0 + ROW_BLOCK, :width]

    def epi_z(slot, r0, c):
        z_ref[r0:r0 + ROW_BLOCK, c:c + COL_CHUNK] = block(slot, r0).astype(BF16)

    def epi_xbc(slot, r0, c):
        w = cw_ref[:, c:c + COL_CHUNK]
        y = cb_ref[:, c:c + COL_CHUNK]
        for k in range(SSD_CONV):
            lo = SUBLANES + r0 - k
            y = y + u_ref[slot, lo:lo + ROW_BLOCK, :] * w[SSD_CONV - 1 - k:SSD_CONV - k]
        xbc_ref[r0:r0 + ROW_BLOCK, c:c + COL_CHUNK] = _silu(y).astype(BF16)

    def epi_q(slot, r0, c):
        blk = block(slot, r0)
        for s in range(0, COL_CHUNK, LANES):
            r = rope(blk[:, s:s + LANES], r0) * q_scale
            qt_ref[c + s:c + s + LANES, r0:r0 + ROW_BLOCK] = r.T.astype(BF16)

    def epi_k(slot, r0, c):
        blk = block(slot, r0)
        for s in range(0, COL_CHUNK, LANES):
            k_ref[r0:r0 + ROW_BLOCK, c + s:c + s + LANES] = rope(blk[:, s:s + LANES], r0).astype(BF16)

    def epi_v(slot, r0, c):
        vt_ref[c:c + COL_CHUNK, r0:r0 + ROW_BLOCK] = block(slot, r0).T.astype(BF16)

    def epi_g(slot, r0, c):
        g_ref[r0:r0 + ROW_BLOCK, c:c + COL_CHUNK] = jax.nn.sigmoid(block(slot, r0)).astype(BF16)

    def epi_dt(slot, r0, c):
        dt_raw = block(slot, r0, LANES) + dtb_ref[...]
        dt_ref[r0:r0 + ROW_BLOCK, :] = (jnp.maximum(dt_raw, 0.0)
                                        + jnp.log1p(jnp.exp(-jnp.abs(dt_raw))))

    stages = []
    for off, total, epi in ((_OFF_XBC, SSD_CONV_DIM, epi_xbc), (_OFF_Q, DA_WIDTH, epi_q),
                            (_OFF_K, DA_WIDTH, epi_k), (_OFF_V, DA_WIDTH, epi_v),
                            (_OFF_G, 2 * D_MODEL, epi_g), (_OFF_Z, SSD_D_INNER, epi_z)):
        stages += [(off + c, COL_CHUNK, epi, c) for c in range(0, total, COL_CHUNK)]
    stages.append((_OFF_DT, LANES, epi_dt, 0))

    def mm_block(s, rb):
        off, width, _, _ = stages[s]
        r0 = rb * ROW_BLOCK
        u_ref[s % 2, SUBLANES + r0:SUBLANES + r0 + ROW_BLOCK, :width] = jnp.dot(
            hb_ref[r0:r0 + ROW_BLOCK, :], w_ref[:, off:off + width], preferred_element_type=F32)

    def load_halo(s):
        _, _, epi, c = stages[s]
        if epi is epi_xbc:
            u_ref[s % 2, 0:SUBLANES, :] = jnp.where(first, 0.0, carry_ref[:, c:c + COL_CHUNK])

    def save_halo(s):
        _, _, epi, c = stages[s]
        if epi is epi_xbc:
            carry_ref[:, c:c + COL_CHUNK] = u_ref[s % 2, rows:rows + SUBLANES, :]

    n_rb = rows // ROW_BLOCK
    load_halo(0)
    for rb in range(n_rb):
        mm_block(0, rb)
    for s in range(len(stages)):
        if s + 1 < len(stages):
            load_halo(s + 1)
        for rb in range(n_rb):
            if s + 1 < len(stages):
                mm_block(s + 1, rb)
            stages[s][2](s % 2, rb * ROW_BLOCK, stages[s][3])
        save_halo(s)


def _rope_tables(seq):
    pos = jnp.arange(seq, dtype=F32)
    inv_freq = jnp.power(ROPE_THETA, -jnp.arange(0, ROPE_DIM, 2, dtype=F32) / ROPE_DIM)
    ang = pos[:, None] * inv_freq[None, :]
    cos, sin = jnp.cos(ang), jnp.sin(ang)
    half = ROPE_DIM // 2
    d = jnp.arange(LANES) % DA_HEAD_DIM
    idx = d % half
    cosf = jnp.where(d[None, :] < ROPE_DIM, cos[:, idx], 1.0)
    sina = jnp.where(d[None, :] < half, -sin[:, idx], 0.0)
    sinb = jnp.where((d[None, :] >= half) & (d[None, :] < ROPE_DIM), sin[:, idx], 0.0)
    return cosf.astype(F32), sina.astype(F32), sinb.astype(F32)


def _inproj(x2, mod, pre_norm_w, w_cat, conv_w, conv_b, dt_bias_pad, seq):
    tokens, d = x2.shape
    tiles_per_seq = seq // ROW_TILE
    cosf, sina, sinb = _rope_tables(seq)

    def row_spec(width):
        return pl.BlockSpec((ROW_TILE, width), lambda i: (i, 0))

    def mod_spec(which):
        return pl.BlockSpec((None, None, 1, d), lambda i: (which, i // tiles_per_seq, 0, 0))

    def rope_spec():
        return pl.BlockSpec((ROW_TILE, LANES), lambda i: (i % tiles_per_seq, 0))

    def transposed_spec():
        return pl.BlockSpec((DA_WIDTH, ROW_TILE), lambda i: (0, i))

    transposed = jax.ShapeDtypeStruct((DA_WIDTH, tokens), BF16)
    out_shapes = (
        jax.ShapeDtypeStruct((tokens, SSD_D_INNER), BF16),
        jax.ShapeDtypeStruct((tokens, SSD_CONV_DIM), BF16),
        transposed,
        jax.ShapeDtypeStruct((tokens, DA_WIDTH), BF16),
        transposed,
        jax.ShapeDtypeStruct((tokens, 2 * D_MODEL), BF16),
        jax.ShapeDtypeStruct((tokens, LANES), F32),
    )
    return pl.pallas_call(
        functools.partial(_inproj_kernel, tiles_per_seq),
        grid=(tokens // ROW_TILE,),
        in_specs=[
            row_spec(d), mod_spec(1), mod_spec(0), _resident((1, d)), _resident((d, _W_COLS)),
            rope_spec(), rope_spec(), rope_spec(),
            _resident((SSD_CONV, SSD_CONV_DIM)), _resident((1, SSD_CONV_DIM)), _resident((1, LANES)),
        ],
        out_specs=[transposed_spec() if s is transposed else row_spec(s.shape[1])
                   for s in out_shapes],
        out_shape=out_shapes,
        scratch_shapes=[
            pltpu.VMEM((SUBLANES, SSD_CONV_DIM), F32),
            pltpu.VMEM((ROW_TILE, d), BF16),
            pltpu.VMEM((2, SUBLANES + ROW_TILE, COL_CHUNK), F32),
        ],
        compiler_params=pltpu.CompilerParams(dimension_semantics=("arbitrary",),
                                             vmem_limit_bytes=VMEM_LIMIT_BYTES),
        name="inproj",
    )(x2, mod, mod, pre_norm_w, w_cat, cosf, sina, sinb, conv_w, conv_b, dt_bias_pad)


def _split3(a):
    hi = a.astype(BF16)
    r = a - hi.astype(F32)
    mid = r.astype(BF16)
    lo = (r - mid.astype(F32)).astype(BF16)
    return hi, mid, lo


def _ssd_kernel(xbc_ref, dt_ref, z_ref, a_ref, dskip_ref, nw_ref, expand_ref, o_ref, state_ref):
    l = SSD_CHUNK
    n = SSD_D_STATE
    gw = SSD_D_INNER // SSD_GROUPS

    @pl.when(pl.program_id(1) == 0)
    def _():
        state_ref[...] = jnp.zeros_like(state_ref)

    xbc = xbc_ref[...]
    xs = xbc[:, :SSD_D_INNER].astype(F32)
    dt = dt_ref[...]
    is_head = lax.broadcasted_iota(jnp.int32, (1, LANES), 1) < SSD_HEADS
    la = dt * jnp.where(is_head, -jnp.exp(a_ref[...]), 0.0)
    row = lax.broadcasted_iota(jnp.int32, (l, l), 0)
    col = lax.broadcasted_iota(jnp.int32, (l, l), 1)
    causal = row >= col
    tril = jnp.where(causal, 1.0, 0.0).astype(BF16)
    expand = expand_ref[...]

    acum = sum(jnp.dot(tril, part, preferred_element_type=F32) for part in _split3(la))
    acum_t = acum.T
    alast = acum[l - 1:l, :]
    dte = jnp.exp(alast - acum)
    dt_x = jnp.dot(dt.astype(BF16), expand, preferred_element_type=F32)
    w_x = jnp.dot((dt * dte).astype(BF16), expand, preferred_element_type=F32)
    cdec = jnp.broadcast_to(jnp.exp(alast), (SUBLANES, LANES))
    cdec_x = sum(jnp.dot(part, expand, preferred_element_type=F32) for part in _split3(cdec))[0:1]

    xd = (xs * dt_x).astype(BF16)
    xw = (xs * w_x).astype(BF16)
    lane = lax.broadcasted_iota(jnp.int32, (l, LANES), 1)

    pairs = []
    for g in range(SSD_GROUPS):
        bm = xbc[:, SSD_D_INNER + g * n:SSD_D_INNER + (g + 1) * n]
        cm = xbc[:, SSD_D_INNER + (SSD_GROUPS + g) * n:SSD_D_INNER + (SSD_GROUPS + g + 1) * n]
        cb = lax.dot_general(cm, bm, (((1,), (1,)), ((), ())), preferred_element_type=F32)
        cmf = cm.astype(F32)
        state = state_ref[g]
        state_b = state.astype(BF16)
        heads_per_group = SSD_HEADS // SSD_GROUPS
        for pr in range(heads_per_group // 2):
            res = []
            for sub in range(2):
                h = g * heads_per_group + 2 * pr + sub
                a_col = acum[:, h:h + 1]
                seg = a_col - acum_t[h:h + 1, :]
                decay = jnp.exp(jnp.where(causal, seg, -jnp.inf))
                lhs = jnp.concatenate([(cb * decay).astype(BF16),
                                       (cmf * jnp.exp(a_col)).astype(BF16)], axis=1)
                c0 = g * gw + pr * LANES
                rhs = jnp.concatenate([xd[:, c0:c0 + LANES],
                                       state_b[:, pr * LANES:(pr + 1) * LANES]], axis=0)
                res.append(jnp.dot(lhs, rhs, preferred_element_type=F32))
            pairs.append(jnp.where(lane < SSD_HEAD_DIM, res[0], res[1]))
        bm_t = bm.astype(F32).T.astype(BF16)
        new = jnp.dot(bm_t, xw[:, g * gw:(g + 1) * gw], preferred_element_type=F32)
        state_ref[g] = state * cdec_x[:, g * gw:(g + 1) * gw] + new

    y = jnp.concatenate(pairs, axis=1)
    y = y + dskip_ref[...] * xs
    y = y * _silu(z_ref[...].astype(F32))
    y = jnp.concatenate([_rms(y[:, g * gw:(g + 1) * gw], SSD_NORM_EPS) for g in range(SSD_GROUPS)],
                        axis=1)
    o_ref[...] = (y * nw_ref[...]).astype(BF16)


def _ssd(xbc, dt, z, a_row, dskip_x, norm_w, bsz, seq):
    tokens = xbc.shape[0]
    nc = seq // SSD_CHUNK
    head = jnp.arange(LANES)[:, None]
    chan = jnp.arange(SSD_D_INNER)[None, :] // SSD_HEAD_DIM
    expand = (head == chan).astype(BF16)

    def row_spec(width):
        return pl.BlockSpec((SSD_CHUNK, width), lambda b, c: (b * nc + c, 0))

    return pl.pallas_call(
        _ssd_kernel,
        grid=(bsz, nc),
        in_specs=[
            row_spec(SSD_CONV_DIM), row_spec(LANES), row_spec(SSD_D_INNER),
            _resident((1, LANES)), _resident((1, SSD_D_INNER)), _resident((1, SSD_D_INNER)),
            _resident((LANES, SSD_D_INNER)),
        ],
        out_specs=row_spec(SSD_D_INNER),
        out_shape=jax.ShapeDtypeStruct((tokens, SSD_D_INNER), BF16),
        scratch_shapes=[pltpu.VMEM((SSD_GROUPS, SSD_D_STATE, SSD_D_INNER // SSD_GROUPS), F32)],
        compiler_params=pltpu.CompilerParams(dimension_semantics=("arbitrary", "arbitrary"),
                                             vmem_limit_bytes=VMEM_LIMIT_BYTES),
        name="ssd",
    )(xbc, dt, z, a_row, dskip_x, norm_w, expand)


def _attn_kernel(lq1_ref, lk1_ref, lq2_ref, lk2_ref, sw_ref, qt_ref, k_ref, vt_ref, o_ref,
                 st_ref, p_ref):
    hw, seq = qt_ref.shape
    t = ATT_BLOCK
    nq = seq // t
    lam = (jnp.exp(jnp.sum(lq1_ref[...] * lk1_ref[...], axis=-1, keepdims=True))
           - jnp.exp(jnp.sum(lq2_ref[...] * lk2_ref[...], axis=-1, keepdims=True)) + LAM_INIT)
    dim = lax.broadcasted_iota(jnp.int32, (hw, t), 0)
    query = lax.broadcasted_iota(jnp.int32, (t, 2 * t), 1)
    query = jnp.where(query >= t, query - t, query)
    key_le_query = lax.broadcasted_iota(jnp.int32, (t, 2 * t), 0) <= query

    def masked_q(i):
        qt = qt_ref[:, i * t:(i + 1) * t]
        zero = jnp.zeros_like(qt)
        return jnp.concatenate([jnp.where(dim < DA_HEAD_DIM, qt, zero),
                                jnp.where(dim >= DA_HEAD_DIM, qt, zero)], axis=1)

    def score_piece(i, j, qm):
        st = jnp.dot(k_ref[j * t:(j + 1) * t, :], qm, preferred_element_type=F32)
        if j == i:
            st = jnp.where(key_le_query, st, -jnp.inf)
        st_ref[i % 2, j * t:(j + 1) * t, :] = st
        return jnp.max(st, axis=0, keepdims=True)

    def softmax_piece(i, j, m):
        e = jnp.exp2(st_ref[i % 2, j * t:(j + 1) * t, :] - m)
        p_ref[i % 2, j * t:(j + 1) * t, :] = e.astype(BF16)
        return jnp.sum(e, axis=0, keepdims=True)

    def finish(i, l):
        n_keys = (i + 1) * t
        acc = jnp.dot(vt_ref[:, :n_keys], p_ref[i % 2, :n_keys, :],
                      preferred_element_type=F32) / l
        d = acc[:, :t] - lam * acc[:, t:]
        d = d * lax.rsqrt(jnp.mean(d * d, axis=0, keepdims=True) + SUBLN_EPS)
        y = (d.T * sw_ref[...]) * (1.0 - LAM_INIT)
        o_ref[i * t:(i + 1) * t, :] = y.astype(BF16)

    m = score_piece(0, 0, masked_q(0))
    for i in range(nq):
        n_next = i + 2 if i + 1 < nq else 0
        qm_next = masked_q(i + 1) if n_next else None
        m_next = None
        l = jnp.zeros((1, 2 * t), F32)
        for j in range(max(i + 1, n_next)):
            if j < n_next:
                mj = score_piece(i + 1, j, qm_next)
                m_next = mj if m_next is None else jnp.maximum(m_next, mj)
            if j <= i:
                l = l + softmax_piece(i, j, m)
        finish(i, l)
        m = m_next


def _attn(qt, k, vt, lq1, lk1, lq2, lk2, subln_w, bsz, seq):
    tokens = k.shape[0]
    hw = 2 * DA_HEAD_DIM
    lam_spec = _resident((1, DA_HEAD_DIM))
    transposed_spec = pl.BlockSpec((hw, seq), lambda b, h: (h, b))
    return pl.pallas_call(
        _attn_kernel,
        grid=(bsz, DA_HEADS),
        in_specs=[
            lam_spec, lam_spec, lam_spec, lam_spec, _resident((1, hw)),
            transposed_spec,
            pl.BlockSpec((seq, hw), lambda b, h: (b, h)),
            transposed_spec,
        ],
        out_specs=pl.BlockSpec((seq, hw), lambda b, h: (b, h)),
        out_shape=jax.ShapeDtypeStruct((tokens, DA_WIDTH), BF16),
        scratch_shapes=[pltpu.VMEM((2, seq, 2 * ATT_BLOCK), F32),
                        pltpu.VMEM((2, seq, 2 * ATT_BLOCK), BF16)],
        compiler_params=pltpu.CompilerParams(dimension_semantics=("arbitrary", "arbitrary"),
                                             vmem_limit_bytes=VMEM_LIMIT_BYTES),
        name="attn",
    )(lq1, lk1, lq2, lk2, subln_w, qt, k, vt)


def _merge_kernel(x_ref, ys_ref, ya_ref, g_ref, gate_ref, nw_ref, wso_ref, wao_ref, wout_ref, o_ref):
    d = x_ref.shape[1]
    so = jnp.dot(ys_ref[...], wso_ref[...], preferred_element_type=F32)
    ao = jnp.dot(ya_ref[...], wao_ref[...], preferred_element_type=F32)
    merged = g_ref[:, :d].astype(F32) * so + g_ref[:, d:].astype(F32) * ao
    mix = jnp.dot(merged.astype(BF16), wout_ref[...], preferred_element_type=F32)
    o_ref[...] = x_ref[...] + gate_ref[...] * (_rms(mix, NORM_EPS) * nw_ref[...])


def _merge(x2, y_ssd, y_att, gates, mod, post_norm_w, w_ssd_o, w_attn_o, w_out, seq):
    tokens, d = x2.shape
    tiles_per_seq = seq // ROW_TILE

    def row_spec(width):
        return pl.BlockSpec((ROW_TILE, width), lambda i: (i, 0))

    return pl.pallas_call(
        _merge_kernel,
        grid=(tokens // ROW_TILE,),
        in_specs=[
            row_spec(d), row_spec(d), row_spec(d), row_spec(2 * d),
            pl.BlockSpec((None, None, 1, d), lambda i: (2, i // tiles_per_seq, 0, 0)),
            _resident((1, d)), _resident((d, d)), _resident((d, d)), _resident((d, d)),
        ],
        out_specs=row_spec(d),
        out_shape=jax.ShapeDtypeStruct((tokens, d), F32),
        compiler_params=pltpu.CompilerParams(dimension_semantics=("arbitrary",),
                                             vmem_limit_bytes=VMEM_LIMIT_BYTES),
        name="merge",
    )(x2, y_ssd, y_att, gates, mod, post_norm_w, w_ssd_o, w_attn_o, w_out)


def _ffn_kernel(tiles_per_seq, x_ref, sc_ref, sh_ref, gate_ref, nw_ref, pw_ref, wup_ref, cw_ref,
                cb_ref, wdn_ref, o_ref, carry_ref, acc_ref, hb_ref, u_ref, act_ref):
    rows = x_ref.shape[0]
    width = 2 * FF_CHUNK
    n_chunks = D_FF // FF_CHUNK
    first = (pl.program_id(0) % tiles_per_seq) == 0
    x = x_ref[...]
    h = (_rms(x, NORM_EPS) * nw_ref[...]) * (1.0 + sc_ref[...]) + sh_ref[...]
    hb_ref[...] = h.astype(BF16)
    k0 = math.sqrt(2.0 / math.pi)

    def up_block(chunk, rb):
        r0 = rb * ROW_BLOCK
        u_ref[chunk % 2, SUBLANES + r0:SUBLANES + r0 + ROW_BLOCK, :] = jnp.dot(
            hb_ref[r0:r0 + ROW_BLOCK, :], wup_ref[:, chunk * width:(chunk + 1) * width],
            preferred_element_type=F32)

    def gate_block(chunk, rb):
        r0 = rb * ROW_BLOCK
        slot = chunk % 2
        w = cw_ref[:, chunk * width:(chunk + 1) * width]
        y = cb_ref[:, chunk * width:(chunk + 1) * width]
        for k in range(FFN_CONV):
            lo = SUBLANES + r0 - k
            y = y + u_ref[slot, lo:lo + ROW_BLOCK, :] * w[FFN_CONV - 1 - k:FFN_CONV - k]
        ug = y[:, :FF_CHUNK]
        uv = y[:, FF_CHUNK:]
        act = 0.5 * ug * (1.0 + jnp.tanh(k0 * (ug + 0.044715 * (ug * ug * ug)))) * uv
        act_ref[slot, r0:r0 + ROW_BLOCK, :] = act.astype(BF16)

    def down_block(chunk, rb):
        r0 = rb * ROW_BLOCK
        part = jnp.dot(act_ref[chunk % 2, r0:r0 + ROW_BLOCK, :],
                       wdn_ref[chunk * FF_CHUNK:(chunk + 1) * FF_CHUNK, :],
                       preferred_element_type=F32)
        if chunk == 0:
            acc_ref[r0:r0 + ROW_BLOCK, :] = part
        else:
            acc_ref[r0:r0 + ROW_BLOCK, :] += part

    def load_halo(chunk):
        u_ref[chunk % 2, 0:SUBLANES, :] = jnp.where(
            first, 0.0, carry_ref[:, chunk * width:(chunk + 1) * width])

    def save_halo(chunk):
        carry_ref[:, chunk * width:(chunk + 1) * width] = u_ref[chunk % 2, rows:rows + SUBLANES, :]

    n_rb = rows // ROW_BLOCK
    load_halo(0)
    for rb in range(n_rb):
        up_block(0, rb)
    for chunk in range(n_chunks + 1):
        if chunk + 1 < n_chunks:
            load_halo(chunk + 1)
        for rb in range(n_rb):
            if chunk + 1 < n_chunks:
                up_block(chunk + 1, rb)
            if chunk >= 1:
                down_block(chunk - 1, rb)
            if chunk < n_chunks:
                gate_block(chunk, rb)
        if chunk < n_chunks:
            save_halo(chunk)

    o_ref[...] = x + gate_ref[...] * (_rms(acc_ref[...], NORM_EPS) * pw_ref[...])


def _chunk_major(a):
    lead = a.shape[:-1]
    g = a[..., :D_FF].reshape(*lead, D_FF // FF_CHUNK, FF_CHUNK)
    v = a[..., D_FF:].reshape(*lead, D_FF // FF_CHUNK, FF_CHUNK)
    return jnp.concatenate([g, v], axis=-1).reshape(*lead, 2 * D_FF)


def _ffn(x1, mod, pre_norm_w, post_norm_w, w_up, conv_w, conv_b, w_down, seq):
    tokens, d = x1.shape
    tiles_per_seq = seq // ROW_TILE

    def row_spec(width):
        return pl.BlockSpec((ROW_TILE, width), lambda i: (i, 0))

    def mod_spec(which):
        return pl.BlockSpec((None, None, 1, d), lambda i: (which, i // tiles_per_seq, 0, 0))

    return pl.pallas_call(
        functools.partial(_ffn_kernel, tiles_per_seq),
        grid=(tokens // ROW_TILE,),
        in_specs=[
            row_spec(d), mod_spec(4), mod_spec(3), mod_spec(5), _resident((1, d)), _resident((1, d)),
            _resident((d, 2 * D_FF)), _resident((FFN_CONV, 2 * D_FF)), _resident((1, 2 * D_FF)),
            _resident((D_FF, d)),
        ],
        out_specs=row_spec(d),
        out_shape=jax.ShapeDtypeStruct((tokens, d), F32),
        scratch_shapes=[
            pltpu.VMEM((SUBLANES, 2 * D_FF), F32),
            pltpu.VMEM((ROW_TILE, d), F32),
            pltpu.VMEM((ROW_TILE, d), BF16),
            pltpu.VMEM((2, SUBLANES + ROW_TILE, 2 * FF_CHUNK), F32),
            pltpu.VMEM((2, ROW_TILE, FF_CHUNK), BF16),
        ],
        compiler_params=pltpu.CompilerParams(dimension_semantics=("arbitrary",),
                                             vmem_limit_bytes=VMEM_LIMIT_BYTES),
        name="ffn",
    )(x1, mod, mod, mod, pre_norm_w, post_norm_w, _chunk_major(w_up), _chunk_major(conv_w),
      _chunk_major(conv_b), w_down)


def kernel(x, c, w_ada, b_ada, pre_norm1_w, w_in, conv_ssd_w, conv_ssd_b, dt_bias, a_log, d_skip,
           ssd_norm_w, w_ssd_o, lambda_q1, lambda_k1, lambda_q2, lambda_k2, subln_w, w_attn_o, w_out,
           post_norm1_w, pre_norm2_w, w_up, conv_ffn_w, conv_ffn_b, w_down, post_norm2_w):
    bsz, seq, d = x.shape
    assert d == D_MODEL and seq % ROW_TILE == 0 and seq % ATT_BLOCK == 0 and seq % SSD_CHUNK == 0
    assert w_ada.shape[0] == 1, "single-layer block"
    layer = 0
    x2 = x.reshape(bsz * seq, d)

    mod = _ada(c, w_ada[layer], b_ada[layer]).reshape(6, bsz, 1, d)

    w = w_in[layer]
    s0 = SSD_D_INNER
    s1 = s0 + SSD_CONV_DIM
    s2 = s1 + SSD_HEADS
    s3 = s2 + DA_WIDTH
    s4 = s3 + DA_WIDTH
    s5 = s4 + DA_WIDTH
    w_cat = jnp.concatenate(
        [w[:, :s0], w[:, s0:s1], w[:, s2:s3], w[:, s3:s4], w[:, s4:s5], w[:, s5:],
         jnp.pad(w[:, s1:s2], ((0, 0), (0, LANES - SSD_HEADS)))], axis=1).astype(BF16)
    dt_bias_pad = jnp.pad(dt_bias[layer], (0, LANES - SSD_HEADS)).reshape(1, LANES)

    z, xbc, qt, k, vt, gates, dt = _inproj(
        x2, mod, pre_norm1_w[layer].reshape(1, d), w_cat, conv_ssd_w[layer],
        conv_ssd_b[layer].reshape(1, -1), dt_bias_pad, seq)

    a_row = jnp.pad(a_log[layer].astype(F32), (0, LANES - SSD_HEADS)).reshape(1, LANES)
    dskip_x = jnp.repeat(d_skip[layer].astype(F32), SSD_HEAD_DIM).reshape(1, SSD_D_INNER)
    y_ssd = _ssd(xbc, dt, z, a_row, dskip_x, ssd_norm_w[layer].reshape(1, -1), bsz, seq)

    y_att = _attn(qt, k, vt, lambda_q1[layer].reshape(1, -1), lambda_k1[layer].reshape(1, -1),
                  lambda_q2[layer].reshape(1, -1), lambda_k2[layer].reshape(1, -1),
                  subln_w[layer].reshape(1, -1), bsz, seq)

    x1 = _merge(x2, y_ssd, y_att, gates, mod, post_norm1_w[layer].reshape(1, d),
                w_ssd_o[layer].astype(BF16), w_attn_o[layer].astype(BF16), w_out[layer].astype(BF16),
                seq)

    out = _ffn(x1, mod, pre_norm2_w[layer].reshape(1, d), post_norm2_w[layer].reshape(1, d),
               w_up[layer].astype(BF16), conv_ffn_w[layer], conv_ffn_b[layer].reshape(1, -1),
               w_down[layer].astype(BF16), seq)
    return out.reshape(bsz, seq, d)
```

```python
import functools
import math

import jax
import jax.numpy as jnp
from jax import lax
from jax.experimental import pallas as pl
from jax.experimental.pallas import tpu as pltpu

F32 = jnp.float32
BF16 = jnp.bfloat16

D_MODEL = 1024
SSD_D_INNER = 1024
SSD_HEAD_DIM = 64
SSD_HEADS = SSD_D_INNER // SSD_HEAD_DIM
SSD_GROUPS = 2
SSD_D_STATE = 128
SSD_CONV = 4
SSD_CHUNK = 128
SSD_CONV_DIM = SSD_D_INNER + 2 * SSD_GROUPS * SSD_D_STATE
SSD_NORM_EPS = 1e-5
DA_HEADS = 8
DA_HEAD_DIM = 64
DA_WIDTH = DA_HEADS * 2 * DA_HEAD_DIM
ROPE_THETA = 500000.0
ROPE_DIM = DA_HEAD_DIM // 4
SUBLN_EPS = 1e-5
D_FF = 2816
FFN_CONV = 3
NORM_EPS = 1e-6
LAM_INIT = 0.8 - 0.6 * math.exp(-0.3 * 0)

LANES = 128
SUBLANES = 8
VMEM_LIMIT_BYTES = 56 * 1024 * 1024

ROW_TILE = 512
FFN_ROW_TILE = 512
COL_CHUNK = 512
FF_CHUNK = 256
ATT_BLOCK = 256
ROW_BLOCK = 128
SSD_BATCH = 2


def _silu(x):
    return x * jax.nn.sigmoid(x)


def _rms(x, eps):
    return x * lax.rsqrt(jnp.mean(x * x, axis=-1, keepdims=True) + eps)


def _resident(shape):
    nd = len(shape)
    return pl.BlockSpec(shape, lambda *_: (0,) * nd, pipeline_mode=pl.Buffered(1))


def _ada_kernel(c_ref, w_ref, b_ref, o_ref):
    c = c_ref[...]
    o_ref[0] = jnp.dot(_silu(c), w_ref[...], preferred_element_type=F32,
                       precision=lax.Precision.HIGHEST) + b_ref[0]


def _ada(c, w_ada, b_ada):
    bsz, d = c.shape
    n_mod = w_ada.shape[1] // d
    return pl.pallas_call(
        _ada_kernel,
        grid=(n_mod,),
        in_specs=[
            pl.BlockSpec((bsz, d), lambda j: (0, 0)),
            pl.BlockSpec((d, d), lambda j: (0, j)),
            pl.BlockSpec((1, 1, d), lambda j: (j, 0, 0)),
        ],
        out_specs=pl.BlockSpec((1, bsz, d), lambda j: (j, 0, 0)),
        out_shape=jax.ShapeDtypeStruct((n_mod, bsz, d), F32),
        compiler_params=pltpu.CompilerParams(dimension_semantics=("arbitrary",),
                                             vmem_limit_bytes=VMEM_LIMIT_BYTES),
        name="ada",
    )(c, w_ada, b_ada.reshape(n_mod, 1, d))


_OFF_Z = 0
_OFF_XBC = _OFF_Z + SSD_D_INNER
_OFF_Q = _OFF_XBC + SSD_CONV_DIM
_OFF_K = _OFF_Q + DA_WIDTH
_OFF_V = _OFF_K + DA_WIDTH
_OFF_G = _OFF_V + DA_WIDTH
_OFF_DT = _OFF_G + 2 * D_MODEL
_W_COLS = _OFF_DT + LANES


def _inproj_kernel(tiles_per_seq, x_ref, sc_ref, sh_ref, nw_ref, w_ref, cosf_ref, sina_ref,
                   sinb_ref, cw_ref, cb_ref, dtb_ref,
                   z_ref, xbc_ref, qt_ref, k_ref, vt_ref, g_ref, dt_ref, carry_ref, hb_ref, u_ref):
    rows = x_ref.shape[0]
    first = (pl.program_id(0) % tiles_per_seq) == 0
    x = x_ref[...]
    h = (_rms(x, NORM_EPS) * nw_ref[...]) * (1.0 + sc_ref[...]) + sh_ref[...]
    hb_ref[...] = h.astype(BF16)
    half = ROPE_DIM // 2
    q_scale = DA_HEAD_DIM ** -0.5 * math.log2(math.e)

    def rope(t, r0):
        rs = slice(r0, r0 + ROW_BLOCK)
        return (t * cosf_ref[rs, :] + pltpu.roll(t, LANES - half, axis=1) * sina_ref[rs, :]
                + pltpu.roll(t, half, axis=1) * sinb_ref[rs, :])

    def block(slot, r0, width=COL_CHUNK):
        return u_ref[slot, SUBLANES + r0:SUBLANES + r0 + ROW_BLOCK, :width]

    def epi_z(slot, r0, c):
        z_ref[r0:r0 + ROW_BLOCK, c:c + COL_CHUNK] = block(slot, r0).astype(BF16)

    def epi_xbc(slot, r0, c):
        w = cw_ref[:, c:c + COL_CHUNK]
        y = cb_ref[:, c:c + COL_CHUNK]
        for k in range(SSD_CONV):
            lo = SUBLANES + r0 - k
            y = y + u_ref[slot, lo:lo + ROW_BLOCK, :] * w[SSD_CONV - 1 - k:SSD_CONV - k]
        xbc_ref[r0:r0 + ROW_BLOCK, c:c + COL_CHUNK] = _silu(y).astype(BF16)

    def epi_q(slot, r0, c):
        blk = block(slot, r0)
        for s in range(0, COL_CHUNK, LANES):
            r = rope(blk[:, s:s + LANES], r0) * q_scale
            qt_ref[c + s:c + s + LANES, r0:r0 + ROW_BLOCK] = r.T.astype(BF16)

    def epi_k(slot, r0, c):
        blk = block(slot, r0)
        for s in range(0, COL_CHUNK, LANES):
            k_ref[r0:r0 + ROW_BLOCK, c + s:c + s + LANES] = rope(blk[:, s:s + LANES], r0).astype(BF16)

    def epi_v(slot, r0, c):
        vt_ref[c:c + COL_CHUNK, r0:r0 + ROW_BLOCK] = block(slot, r0).T.astype(BF16)

    def epi_g(slot, r0, c):
        g_ref[r0:r0 + ROW_BLOCK, c:c + COL_CHUNK] = jax.nn.sigmoid(block(slot, r0)).astype(BF16)

    def epi_dt(slot, r0, c):
        dt_raw = block(slot, r0, LANES) + dtb_ref[...]
        dt_ref[r0:r0 + ROW_BLOCK, :] = (jnp.maximum(dt_raw, 0.0)
                                        + jnp.log1p(jnp.exp(-jnp.abs(dt_raw))))

    stages = []
    for off, total, epi in ((_OFF_XBC, SSD_CONV_DIM, epi_xbc), (_OFF_Q, DA_WIDTH, epi_q),
                            (_OFF_K, DA_WIDTH, epi_k), (_OFF_V, DA_WIDTH, epi_v),
                            (_OFF_G, 2 * D_MODEL, epi_g), (_OFF_Z, SSD_D_INNER, epi_z)):
        stages += [(off + c, COL_CHUNK, epi, c) for c in range(0, total, COL_CHUNK)]
    stages.append((_OFF_DT, LANES, epi_dt, 0))

    def mm_block(s, rb):
        off, width, _, _ = stages[s]
        r0 = rb * ROW_BLOCK
        u_ref[s % 2, SUBLANES + r0:SUBLANES + r0 + ROW_BLOCK, :width] = jnp.dot(
            hb_ref[r0:r0 + ROW_BLOCK, :], w_ref[:, off:off + width], preferred_element_type=F32)

    def load_halo(s):
        _, _, epi, c = stages[s]
        if epi is epi_xbc:
            u_ref[s % 2, 0:SUBLANES, :] = jnp.where(first, 0.0, carry_ref[:, c:c + COL_CHUNK])

    def save_halo(s):
        _, _, epi, c = stages[s]
        if epi is epi_xbc:
            carry_ref[:, c:c + COL_CHUNK] = u_ref[s % 2, rows:rows + SUBLANES, :]

    n_rb = rows // ROW_BLOCK
    load_halo(0)
    for rb in range(n_rb):
        mm_block(0, rb)
    for s in range(len(stages)):
        if s + 1 < len(stages):
            load_halo(s + 1)
        for rb in range(n_rb):
            if s + 1 < len(stages):
                mm_block(s + 1, rb)
            stages[s][2](s % 2, rb * ROW_BLOCK, stages[s][3])
        save_halo(s)


def _rope_tables(seq):
    pos = jnp.arange(seq, dtype=F32)
    inv_freq = jnp.power(ROPE_THETA, -jnp.arange(0, ROPE_DIM, 2, dtype=F32) / ROPE_DIM)
    ang = pos[:, None] * inv_freq[None, :]
    cos, sin = jnp.cos(ang), jnp.sin(ang)
    half = ROPE_DIM // 2
    d = jnp.arange(LANES) % DA_HEAD_DIM
    idx = d % half
    cosf = jnp.where(d[None, :] < ROPE_DIM, cos[:, idx], 1.0)
    sina = jnp.where(d[None, :] < half, -sin[:, idx], 0.0)
    sinb = jnp.where((d[None, :] >= half) & (d[None, :] < ROPE_DIM), sin[:, idx], 0.0)
    return cosf.astype(F32), sina.astype(F32), sinb.astype(F32)


def _inproj(x2, mod, pre_norm_w, w_cat, conv_w, conv_b, dt_bias_pad, seq):
    tokens, d = x2.shape
    tiles_per_seq = seq // ROW_TILE
    cosf, sina, sinb = _rope_tables(seq)

    def row_spec(width):
        return pl.BlockSpec((ROW_TILE, width), lambda i: (i, 0))

    def mod_spec(which):
        return pl.BlockSpec((None, None, 1, d), lambda i: (which, i // tiles_per_seq, 0, 0))

    def rope_spec():
        return pl.BlockSpec((ROW_TILE, LANES), lambda i: (i % tiles_per_seq, 0))

    def transposed_spec():
        return pl.BlockSpec((DA_WIDTH, ROW_TILE), lambda i: (0, i))

    transposed = jax.ShapeDtypeStruct((DA_WIDTH, tokens), BF16)
    out_shapes = (
        jax.ShapeDtypeStruct((tokens, SSD_D_INNER), BF16),
        jax.ShapeDtypeStruct((tokens, SSD_CONV_DIM), BF16),
        transposed,
        jax.ShapeDtypeStruct((tokens, DA_WIDTH), BF16),
        transposed,
        jax.ShapeDtypeStruct((tokens, 2 * D_MODEL), BF16),
        jax.ShapeDtypeStruct((tokens, LANES), F32),
    )
    return pl.pallas_call(
        functools.partial(_inproj_kernel, tiles_per_seq),
        grid=(tokens // ROW_TILE,),
        in_specs=[
            row_spec(d), mod_spec(1), mod_spec(0), _resident((1, d)), _resident((d, _W_COLS)),
            rope_spec(), rope_spec(), rope_spec(),
            _resident((SSD_CONV, SSD_CONV_DIM)), _resident((1, SSD_CONV_DIM)), _resident((1, LANES)),
        ],
        out_specs=[transposed_spec() if s is transposed else row_spec(s.shape[1])
                   for s in out_shapes],
        out_shape=out_shapes,
        scratch_shapes=[
            pltpu.VMEM((SUBLANES, SSD_CONV_DIM), F32),
            pltpu.VMEM((ROW_TILE, d), BF16),
            pltpu.VMEM((2, SUBLANES + ROW_TILE, COL_CHUNK), F32),
        ],
        compiler_params=pltpu.CompilerParams(dimension_semantics=("arbitrary",),
                                             vmem_limit_bytes=VMEM_LIMIT_BYTES),
        name="inproj",
    )(x2, mod, mod, pre_norm_w, w_cat, cosf, sina, sinb, conv_w, conv_b, dt_bias_pad)


def _split3(a):
    hi = a.astype(BF16)
    r = a - hi.astype(F32)
    mid = r.astype(BF16)
    lo = (r - mid.astype(F32)).astype(BF16)
    return hi, mid, lo


def _ssd_kernel(xbc_ref, dt_ref, z_ref, a_ref, dskip_ref, nw_ref, expand_ref, o_ref, state_ref):
    l = SSD_CHUNK
    n = SSD_D_STATE
    gw = SSD_D_INNER // SSD_GROUPS
    heads_per_group = SSD_HEADS // SSD_GROUPS

    @pl.when(pl.program_id(1) == 0)
    def _():
        state_ref[...] = jnp.zeros_like(state_ref)

    is_head = lax.broadcasted_iota(jnp.int32, (1, LANES), 1) < SSD_HEADS
    a_scaled = jnp.where(is_head, -jnp.exp(a_ref[...]) * math.log2(math.e), 0.0)
    row = lax.broadcasted_iota(jnp.int32, (l, l), 0)
    col = lax.broadcasted_iota(jnp.int32, (l, l), 1)
    causal = row >= col
    tril = jnp.where(causal, 1.0, 0.0).astype(BF16)
    lane = lax.broadcasted_iota(jnp.int32, (l, LANES), 1)

    def program(s):
        xbc = xbc_ref[s]
        xs = xbc[:, :SSD_D_INNER].astype(F32)
        dt = dt_ref[s]
        la = dt * a_scaled
        expand = expand_ref[...]
        acum = sum(jnp.dot(tril, part, preferred_element_type=F32) for part in _split3(la))
        yield
        acum_t = acum.T
        alast = acum[l - 1:l, :]
        dte = jnp.exp2(alast - acum)
        dt_x = jnp.dot(dt.astype(BF16), expand, preferred_element_type=F32)
        yield
        w_x = jnp.dot((dt * dte).astype(BF16), expand, preferred_element_type=F32)
        cdec = jnp.broadcast_to(jnp.exp2(alast), (SUBLANES, LANES))
        cdec_x = sum(jnp.dot(part, expand, preferred_element_type=F32) for part in _split3(cdec))[0:1]
        yield
        xd = (xs * dt_x).astype(BF16)
        xw = (xs * w_x).astype(BF16)
        yield
        pairs = []
        for g in range(SSD_GROUPS):
            bm = xbc[:, SSD_D_INNER + g * n:SSD_D_INNER + (g + 1) * n]
            cm = xbc[:, SSD_D_INNER + (SSD_GROUPS + g) * n:SSD_D_INNER + (SSD_GROUPS + g + 1) * n]
            cb = lax.dot_general(cm, bm, (((1,), (1,)), ((), ())), preferred_element_type=F32)
            cmf = cm.astype(F32)
            state = state_ref[s, g]
            state_b = state.astype(BF16)
            for pr in range(heads_per_group // 2):
                res = []
                for sub in range(2):
                    h = g * heads_per_group + 2 * pr + sub
                    a_col = jnp.broadcast_to(acum[:, h:h + 1], (l, l))
                    seg = a_col - acum_t[h:h + 1, :]
                    decay = jnp.exp2(jnp.where(causal, seg, -jnp.inf))
                    lhs = jnp.concatenate([(cb * decay).astype(BF16),
                                           (cmf * jnp.exp2(a_col)).astype(BF16)], axis=1)
                    c0 = g * gw + pr * LANES
                    rhs = jnp.concatenate([xd[:, c0:c0 + LANES],
                                           state_b[:, pr * LANES:(pr + 1) * LANES]], axis=0)
                    res.append(jnp.dot(lhs, rhs, preferred_element_type=F32))
                    yield
                pairs.append(jnp.where(lane < SSD_HEAD_DIM, res[0], res[1]))
            bm_t = bm.astype(F32).T.astype(BF16)
            new = jnp.dot(bm_t, xw[:, g * gw:(g + 1) * gw], preferred_element_type=F32)
            state_ref[s, g] = state * cdec_x[:, g * gw:(g + 1) * gw] + new
            yield
        y = jnp.concatenate(pairs, axis=1)
        y = y + dskip_ref[...] * xs
        y = y * _silu(z_ref[s].astype(F32))
        yield
        y = jnp.concatenate([_rms(y[:, g * gw:(g + 1) * gw], SSD_NORM_EPS)
                             for g in range(SSD_GROUPS)], axis=1)
        o_ref[s] = (y * nw_ref[...]).astype(BF16)

    programs = [program(s) for s in range(SSD_BATCH)]
    while programs:
        for p in list(programs):
            if next(p, programs) is programs:
                programs.remove(p)


def _ssd(xbc, dt, z, a_row, dskip_x, norm_w, bsz, seq):
    nc = seq // SSD_CHUNK
    head = jnp.arange(LANES)[:, None]
    chan = jnp.arange(SSD_D_INNER)[None, :] // SSD_HEAD_DIM
    expand = (head == chan).astype(BF16)

    def seq_spec(width):
        return pl.BlockSpec((SSD_BATCH, SSD_CHUNK, width), lambda b, c: (b, c, 0))

    out = pl.pallas_call(
        _ssd_kernel,
        grid=(bsz // SSD_BATCH, nc),
        in_specs=[
            seq_spec(SSD_CONV_DIM), seq_spec(LANES), seq_spec(SSD_D_INNER),
            _resident((1, LANES)), _resident((1, SSD_D_INNER)), _resident((1, SSD_D_INNER)),
            _resident((LANES, SSD_D_INNER)),
        ],
        out_specs=seq_spec(SSD_D_INNER),
        out_shape=jax.ShapeDtypeStruct((bsz, seq, SSD_D_INNER), BF16),
        scratch_shapes=[pltpu.VMEM((SSD_BATCH, SSD_GROUPS, SSD_D_STATE, SSD_D_INNER // SSD_GROUPS),
                                   F32)],
        compiler_params=pltpu.CompilerParams(dimension_semantics=("arbitrary", "arbitrary"),
                                             vmem_limit_bytes=VMEM_LIMIT_BYTES),
        name="ssd",
    )(xbc.reshape(bsz, seq, -1), dt.reshape(bsz, seq, -1), z.reshape(bsz, seq, -1),
      a_row, dskip_x, norm_w, expand)
    return out.reshape(bsz * seq, SSD_D_INNER)


def _attn_kernel(lq1_ref, lk1_ref, lq2_ref, lk2_ref, sw_ref, qt_ref, k_ref, vt_ref, o_ref,
                 st_ref, p_ref):
    hw, seq = qt_ref.shape
    t = ATT_BLOCK
    nq = seq // t
    lam = (jnp.exp(jnp.sum(lq1_ref[...] * lk1_ref[...], axis=-1, keepdims=True))
           - jnp.exp(jnp.sum(lq2_ref[...] * lk2_ref[...], axis=-1, keepdims=True)) + LAM_INIT)
    dim = lax.broadcasted_iota(jnp.int32, (hw, t), 0)
    query = lax.broadcasted_iota(jnp.int32, (t, 2 * t), 1)
    query = jnp.where(query >= t, query - t, query)
    key_le_query = lax.broadcasted_iota(jnp.int32, (t, 2 * t), 0) <= query

    def masked_q(i):
        qt = qt_ref[:, i * t:(i + 1) * t]
        zero = jnp.zeros_like(qt)
        return jnp.concatenate([jnp.where(dim < DA_HEAD_DIM, qt, zero),
                                jnp.where(dim >= DA_HEAD_DIM, qt, zero)], axis=1)

    def score_piece(i, j, qm):
        st = jnp.dot(k_ref[j * t:(j + 1) * t, :], qm, preferred_element_type=F32)
        if j == i:
            st = jnp.where(key_le_query, st, -jnp.inf)
        st_ref[i % 2, j * t:(j + 1) * t, :] = st
        return jnp.max(st, axis=0, keepdims=True)

    def softmax_piece(i, j, m):
        e = jnp.exp2(st_ref[i % 2, j * t:(j + 1) * t, :] - m)
        p_ref[i % 2, j * t:(j + 1) * t, :] = e.astype(BF16)
        return jnp.sum(e, axis=0, keepdims=True)

    def pv_piece(i, j):
        return jnp.dot(vt_ref[:, j * t:(j + 1) * t], p_ref[i % 2, j * t:(j + 1) * t, :],
                       preferred_element_type=F32)

    def finish(i, acc, l):
        acc = acc / l
        d = acc[:, :t] - lam * acc[:, t:]
        d = d * lax.rsqrt(jnp.mean(d * d, axis=0, keepdims=True) + SUBLN_EPS)
        y = (d.T * sw_ref[...]) * (1.0 - LAM_INIT)
        o_ref[i * t:(i + 1) * t, :] = y.astype(BF16)

    m = score_piece(0, 0, masked_q(0))
    l_prev = None
    for i in range(nq + 1):
        n_score = i + 2 if i + 1 < nq else 0
        n_soft = i + 1 if i < nq else 0
        n_pv = i if i >= 1 else 0
        qm_next = masked_q(i + 1) if n_score else None
        m_next = None
        acc = None
        l = jnp.zeros((1, 2 * t), F32)
        for j in range(max(n_score, n_soft, n_pv)):
            if j < n_score:
                mj = score_piece(i + 1, j, qm_next)
                m_next = mj if m_next is None else jnp.maximum(m_next, mj)
            if j < n_pv:
                part = pv_piece(i - 1, j)
                acc = part if acc is None else acc + part
            if j < n_soft:
                l = l + softmax_piece(i, j, m)
        if n_pv:
            finish(i - 1, acc, l_prev)
        m, l_prev = m_next, l


def _attn(qt, k, vt, lq1, lk1, lq2, lk2, subln_w, bsz, seq):
    tokens = k.shape[0]
    hw = 2 * DA_HEAD_DIM
    lam_spec = _resident((1, DA_HEAD_DIM))
    transposed_spec = pl.BlockSpec((hw, seq), lambda b, h: (h, b))
    return pl.pallas_call(
        _attn_kernel,
        grid=(bsz, DA_HEADS),
        in_specs=[
            lam_spec, lam_spec, lam_spec, lam_spec, _resident((1, hw)),
            transposed_spec,
            pl.BlockSpec((seq, hw), lambda b, h: (b, h)),
            transposed_spec,
        ],
        out_specs=pl.BlockSpec((seq, hw), lambda b, h: (b, h)),
        out_shape=jax.ShapeDtypeStruct((tokens, DA_WIDTH), BF16),
        scratch_shapes=[pltpu.VMEM((2, seq, 2 * ATT_BLOCK), F32),
                        pltpu.VMEM((2, seq, 2 * ATT_BLOCK), BF16)],
        compiler_params=pltpu.CompilerParams(dimension_semantics=("arbitrary", "arbitrary"),
                                             vmem_limit_bytes=VMEM_LIMIT_BYTES),
        name="attn",
    )(lq1, lk1, lq2, lk2, subln_w, qt, k, vt)


def _merge_kernel(x_ref, ys_ref, ya_ref, g_ref, gate_ref, nw_ref, wso_ref, wao_ref, wout_ref, o_ref):
    d = x_ref.shape[1]
    so = jnp.dot(ys_ref[...], wso_ref[...], preferred_element_type=F32)
    ao = jnp.dot(ya_ref[...], wao_ref[...], preferred_element_type=F32)
    merged = g_ref[:, :d].astype(F32) * so + g_ref[:, d:].astype(F32) * ao
    mix = jnp.dot(merged.astype(BF16), wout_ref[...], preferred_element_type=F32)
    o_ref[...] = x_ref[...] + gate_ref[...] * (_rms(mix, NORM_EPS) * nw_ref[...])


def _merge(x2, y_ssd, y_att, gates, mod, post_norm_w, w_ssd_o, w_attn_o, w_out, seq):
    tokens, d = x2.shape
    tiles_per_seq = seq // ROW_TILE

    def row_spec(width):
        return pl.BlockSpec((ROW_TILE, width), lambda i: (i, 0))

    return pl.pallas_call(
        _merge_kernel,
        grid=(tokens // ROW_TILE,),
        in_specs=[
            row_spec(d), row_spec(d), row_spec(d), row_spec(2 * d),
            pl.BlockSpec((None, None, 1, d), lambda i: (2, i // tiles_per_seq, 0, 0)),
            _resident((1, d)), _resident((d, d)), _resident((d, d)), _resident((d, d)),
        ],
        out_specs=row_spec(d),
        out_shape=jax.ShapeDtypeStruct((tokens, d), F32),
        compiler_params=pltpu.CompilerParams(dimension_semantics=("arbitrary",),
                                             vmem_limit_bytes=VMEM_LIMIT_BYTES),
        name="merge",
    )(x2, y_ssd, y_att, gates, mod, post_norm_w, w_ssd_o, w_attn_o, w_out)


def _ffn_kernel(tiles_per_seq, x_ref, sc_ref, sh_ref, gate_ref, nw_ref, pw_ref, wup_ref, cw_ref,
                cb_ref, wdn_ref, o_ref, carry_ref, acc_ref, hb_ref, u_ref, act_ref):
    rows = x_ref.shape[0]
    width = 2 * FF_CHUNK
    n_chunks = D_FF // FF_CHUNK
    first = (pl.program_id(0) % tiles_per_seq) == 0
    x = x_ref[...]
    h = (_rms(x, NORM_EPS) * nw_ref[...]) * (1.0 + sc_ref[...]) + sh_ref[...]
    hb_ref[...] = h.astype(BF16)
    k0 = math.sqrt(2.0 / math.pi)

    def up_block(chunk, rb):
        r0 = rb * ROW_BLOCK
        u_ref[chunk % 2, SUBLANES + r0:SUBLANES + r0 + ROW_BLOCK, :] = jnp.dot(
            hb_ref[r0:r0 + ROW_BLOCK, :], wup_ref[:, chunk * width:(chunk + 1) * width],
            preferred_element_type=F32)

    def gate_block(chunk, rb):
        r0 = rb * ROW_BLOCK
        slot = chunk % 2
        w = cw_ref[:, chunk * width:(chunk + 1) * width]
        y = cb_ref[:, chunk * width:(chunk + 1) * width]
        for k in range(FFN_CONV):
            lo = SUBLANES + r0 - k
            y = y + u_ref[slot, lo:lo + ROW_BLOCK, :] * w[FFN_CONV - 1 - k:FFN_CONV - k]
        ug = y[:, :FF_CHUNK]
        uv = y[:, FF_CHUNK:]
        act = 0.5 * ug * (1.0 + jnp.tanh(k0 * (ug + 0.044715 * (ug * ug * ug)))) * uv
        act_ref[slot, r0:r0 + ROW_BLOCK, :] = act.astype(BF16)

    def down_block(chunk, rb):
        r0 = rb * ROW_BLOCK
        part = jnp.dot(act_ref[chunk % 2, r0:r0 + ROW_BLOCK, :],
                       wdn_ref[chunk * FF_CHUNK:(chunk + 1) * FF_CHUNK, :],
                       preferred_element_type=F32)
        if chunk == 0:
            acc_ref[r0:r0 + ROW_BLOCK, :] = part
        else:
            acc_ref[r0:r0 + ROW_BLOCK, :] += part

    def load_halo(chunk):
        u_ref[chunk % 2, 0:SUBLANES, :] = jnp.where(
            first, 0.0, carry_ref[:, chunk * width:(chunk + 1) * width])

    def save_halo(chunk):
        carry_ref[:, chunk * width:(chunk + 1) * width] = u_ref[chunk % 2, rows:rows + SUBLANES, :]

    n_rb = rows // ROW_BLOCK
    load_halo(0)
    for rb in range(n_rb):
        up_block(0, rb)
    for chunk in range(n_chunks + 1):
        if chunk + 1 < n_chunks:
            load_halo(chunk + 1)
        for rb in range(n_rb):
            if chunk + 1 < n_chunks:
                up_block(chunk + 1, rb)
            if chunk >= 1:
                down_block(chunk - 1, rb)
            if chunk < n_chunks:
                gate_block(chunk, rb)
        if chunk < n_chunks:
            save_halo(chunk)

    o_ref[...] = x + gate_ref[...] * (_rms(acc_ref[...], NORM_EPS) * pw_ref[...])


def _chunk_major(a):
    lead = a.shape[:-1]
    g = a[..., :D_FF].reshape(*lead, D_FF // FF_CHUNK, FF_CHUNK)
    v = a[..., D_FF:].reshape(*lead, D_FF // FF_CHUNK, FF_CHUNK)
    return jnp.concatenate([g, v], axis=-1).reshape(*lead, 2 * D_FF)


def _ffn(x1, mod, pre_norm_w, post_norm_w, w_up, conv_w, conv_b, w_down, seq):
    tokens, d = x1.shape
    rows = FFN_ROW_TILE
    tiles_per_seq = seq // rows

    def row_spec(width):
        return pl.BlockSpec((rows, width), lambda i: (i, 0))

    def mod_spec(which):
        return pl.BlockSpec((None, None, 1, d), lambda i: (which, i // tiles_per_seq, 0, 0))

    return pl.pallas_call(
        functools.partial(_ffn_kernel, tiles_per_seq),
        grid=(tokens // rows,),
        in_specs=[
            row_spec(d), mod_spec(4), mod_spec(3), mod_spec(5), _resident((1, d)), _resident((1, d)),
            _resident((d, 2 * D_FF)), _resident((FFN_CONV, 2 * D_FF)), _resident((1, 2 * D_FF)),
            _resident((D_FF, d)),
        ],
        out_specs=row_spec(d),
        out_shape=jax.ShapeDtypeStruct((tokens, d), F32),
        scratch_shapes=[
            pltpu.VMEM((SUBLANES, 2 * D_FF), F32),
            pltpu.VMEM((rows, d), F32),
            pltpu.VMEM((rows, d), BF16),
            pltpu.VMEM((2, SUBLANES + rows, 2 * FF_CHUNK), F32),
            pltpu.VMEM((2, rows, FF_CHUNK), BF16),
        ],
        compiler_params=pltpu.CompilerParams(dimension_semantics=("arbitrary",),
                                             vmem_limit_bytes=VMEM_LIMIT_BYTES),
        name="ffn",
    )(x1, mod, mod, mod, pre_norm_w, post_norm_w, _chunk_major(w_up), _chunk_major(conv_w),
      _chunk_major(conv_b), w_down)


def kernel(x, c, w_ada, b_ada, pre_norm1_w, w_in, conv_ssd_w, conv_ssd_b, dt_bias, a_log, d_skip,
           ssd_norm_w, w_ssd_o, lambda_q1, lambda_k1, lambda_q2, lambda_k2, subln_w, w_attn_o, w_out,
           post_norm1_w, pre_norm2_w, w_up, conv_ffn_w, conv_ffn_b, w_down, post_norm2_w):
    bsz, seq, d = x.shape
    assert d == D_MODEL and seq % ROW_TILE == 0 and seq % ATT_BLOCK == 0 and seq % SSD_CHUNK == 0
    assert seq % FFN_ROW_TILE == 0 and bsz % SSD_BATCH == 0
    assert w_ada.shape[0] == 1, "single-layer block"
    layer = 0
    x2 = x.reshape(bsz * seq, d)

    mod = _ada(c, w_ada[layer], b_ada[layer]).reshape(6, bsz, 1, d)

    w = w_in[layer].astype(BF16)
    s0 = SSD_D_INNER
    s1 = s0 + SSD_CONV_DIM
    s2 = s1 + SSD_HEADS
    s3 = s2 + DA_WIDTH
    s4 = s3 + DA_WIDTH
    s5 = s4 + DA_WIDTH
    w_cat = jnp.concatenate(
        [w[:, :s0], w[:, s0:s1], w[:, s2:s3], w[:, s3:s4], w[:, s4:s5], w[:, s5:],
         jnp.pad(w[:, s1:s2], ((0, 0), (0, LANES - SSD_HEADS)))], axis=1)
    dt_bias_pad = jnp.pad(dt_bias[layer], (0, LANES - SSD_HEADS)).reshape(1, LANES)

    z, xbc, qt, k, vt, gates, dt = _inproj(
        x2, mod, pre_norm1_w[layer].reshape(1, d), w_cat, conv_ssd_w[layer],
        conv_ssd_b[layer].reshape(1, -1), dt_bias_pad, seq)

    a_row = jnp.pad(a_log[layer].astype(F32), (0, LANES - SSD_HEADS)).reshape(1, LANES)
    dskip_x = jnp.repeat(d_skip[layer].astype(F32), SSD_HEAD_DIM).reshape(1, SSD_D_INNER)
    y_ssd = _ssd(xbc, dt, z, a_row, dskip_x, ssd_norm_w[layer].reshape(1, -1), bsz, seq)

    y_att = _attn(qt, k, vt, lambda_q1[layer].reshape(1, -1), lambda_k1[layer].reshape(1, -1),
                  lambda_q2[layer].reshape(1, -1), lambda_k2[layer].reshape(1, -1),
                  subln_w[layer].reshape(1, -1), bsz, seq)

    x1 = _merge(x2, y_ssd, y_att, gates, mod, post_norm1_w[layer].reshape(1, d),
                w_ssd_o[layer].astype(BF16), w_attn_o[layer].astype(BF16), w_out[layer].astype(BF16),
                seq)

    out = _ffn(x1, mod, pre_norm2_w[layer].reshape(1, d), post_norm2_w[layer].reshape(1, d),
               w_up[layer].astype(BF16), conv_ffn_w[layer], conv_ffn_b[layer].reshape(1, -1),
               w_down[layer].astype(BF16), seq)
    return out.reshape(bsz, seq, d)
```

```python
import functools
import math

import jax
import jax.numpy as jnp
from jax import lax
from jax.experimental import pallas as pl
from jax.experimental.pallas import tpu as pltpu

F32 = jnp.float32
BF16 = jnp.bfloat16

D_MODEL = 1024
SSD_D_INNER = 1024
SSD_HEAD_DIM = 64
SSD_HEADS = SSD_D_INNER // SSD_HEAD_DIM
SSD_GROUPS = 2
SSD_D_STATE = 128
SSD_CONV = 4
SSD_CHUNK = 128
SSD_CONV_DIM = SSD_D_INNER + 2 * SSD_GROUPS * SSD_D_STATE
SSD_NORM_EPS = 1e-5
DA_HEADS = 8
DA_HEAD_DIM = 64
DA_WIDTH = DA_HEADS * 2 * DA_HEAD_DIM
ROPE_THETA = 500000.0
ROPE_DIM = DA_HEAD_DIM // 4
SUBLN_EPS = 1e-5
D_FF = 2816
FFN_CONV = 3
NORM_EPS = 1e-6
LAM_INIT = 0.8 - 0.6 * math.exp(-0.3 * 0)

LANES = 128
SUBLANES = 8
VMEM_LIMIT_BYTES = 56 * 1024 * 1024

ROW_TILE = 512
FFN_ROW_TILE = 512
COL_CHUNK = 512
FF_CHUNK = 256
ATT_BLOCK = 256
ROW_BLOCK = 128
SSD_BATCH = 2
ATT_HEADS = 2


def _silu(x):
    return x * jax.nn.sigmoid(x)


def _rms(x, eps):
    return x * lax.rsqrt(jnp.mean(x * x, axis=-1, keepdims=True) + eps)


def _resident(shape):
    nd = len(shape)
    return pl.BlockSpec(shape, lambda *_: (0,) * nd, pipeline_mode=pl.Buffered(1))


def _ada_kernel(c_ref, w_ref, b_ref, o_ref):
    c = c_ref[...]
    o_ref[0] = jnp.dot(_silu(c), w_ref[...], preferred_element_type=F32,
                       precision=lax.Precision.HIGHEST) + b_ref[0]


def _ada(c, w_ada, b_ada):
    bsz, d = c.shape
    n_mod = w_ada.shape[1] // d
    return pl.pallas_call(
        _ada_kernel,
        grid=(n_mod,),
        in_specs=[
            pl.BlockSpec((bsz, d), lambda j: (0, 0)),
            pl.BlockSpec((d, d), lambda j: (0, j)),
            pl.BlockSpec((1, 1, d), lambda j: (j, 0, 0)),
        ],
        out_specs=pl.BlockSpec((1, bsz, d), lambda j: (j, 0, 0)),
        out_shape=jax.ShapeDtypeStruct((n_mod, bsz, d), F32),
        compiler_params=pltpu.CompilerParams(dimension_semantics=("arbitrary",),
                                             vmem_limit_bytes=VMEM_LIMIT_BYTES),
        name="ada",
    )(c, w_ada, b_ada.reshape(n_mod, 1, d))


_OFF_Z = 0
_OFF_XBC = _OFF_Z + SSD_D_INNER
_OFF_Q = _OFF_XBC + SSD_CONV_DIM
_OFF_K = _OFF_Q + DA_WIDTH
_OFF_V = _OFF_K + DA_WIDTH
_OFF_G = _OFF_V + DA_WIDTH
_OFF_DT = _OFF_G + 2 * D_MODEL
_W_COLS = _OFF_DT + LANES


def _inproj_kernel(tiles_per_seq, x_ref, sc_ref, sh_ref, nw_ref, w_ref, cosf_ref, sina_ref,
                   sinb_ref, cw_ref, cb_ref, dtb_ref,
                   z_ref, xbc_ref, qt_ref, k_ref, vt_ref, g_ref, dt_ref, carry_ref, hb_ref, u_ref):
    rows = x_ref.shape[0]
    first = (pl.program_id(0) % tiles_per_seq) == 0
    x = x_ref[...]
    h = (_rms(x, NORM_EPS) * nw_ref[...]) * (1.0 + sc_ref[...]) + sh_ref[...]
    hb_ref[...] = h.astype(BF16)
    half = ROPE_DIM // 2
    q_scale = DA_HEAD_DIM ** -0.5 * math.log2(math.e)

    def rope(t, r0):
        rs = slice(r0, r0 + ROW_BLOCK)
        return (t * cosf_ref[rs, :] + pltpu.roll(t, LANES - half, axis=1) * sina_ref[rs, :]
                + pltpu.roll(t, half, axis=1) * sinb_ref[rs, :])

    def block(slot, r0, width=COL_CHUNK):
        return u_ref[slot, SUBLANES + r0:SUBLANES + r0 + ROW_BLOCK, :width]

    def epi_z(slot, r0, c):
        z_ref[r0:r0 + ROW_BLOCK, c:c + COL_CHUNK] = block(slot, r0).astype(BF16)

    def epi_xbc(slot, r0, c):
        w = cw_ref[:, c:c + COL_CHUNK]
        y = cb_ref[:, c:c + COL_CHUNK]
        for k in range(SSD_CONV):
            lo = SUBLANES + r0 - k
            y = y + u_ref[slot, lo:lo + ROW_BLOCK, :] * w[SSD_CONV - 1 - k:SSD_CONV - k]
        xbc_ref[r0:r0 + ROW_BLOCK, c:c + COL_CHUNK] = _silu(y).astype(BF16)

    def epi_q(slot, r0, c):
        blk = block(slot, r0)
        for s in range(0, COL_CHUNK, LANES):
            r = rope(blk[:, s:s + LANES], r0) * q_scale
            qt_ref[c + s:c + s + LANES, r0:r0 + ROW_BLOCK] = r.T.astype(BF16)

    def epi_k(slot, r0, c):
        blk = block(slot, r0)
        for s in range(0, COL_CHUNK, LANES):
            k_ref[r0:r0 + ROW_BLOCK, c + s:c + s + LANES] = rope(blk[:, s:s + LANES], r0).astype(BF16)

    def epi_v(slot, r0, c):
        vt_ref[c:c + COL_CHUNK, r0:r0 + ROW_BLOCK] = block(slot, r0).T.astype(BF16)

    def epi_g(slot, r0, c):
        g_ref[r0:r0 + ROW_BLOCK, c:c + COL_CHUNK] = jax.nn.sigmoid(block(slot, r0)).astype(BF16)

    def epi_dt(slot, r0, c):
        dt_raw = block(slot, r0, LANES) + dtb_ref[...]
        dt_ref[r0:r0 + ROW_BLOCK, :] = (jnp.maximum(dt_raw, 0.0)
                                        + jnp.log1p(jnp.exp(-jnp.abs(dt_raw))))

    stages = []
    for off, total, epi in ((_OFF_XBC, SSD_CONV_DIM, epi_xbc), (_OFF_Q, DA_WIDTH, epi_q),
                            (_OFF_K, DA_WIDTH, epi_k), (_OFF_V, DA_WIDTH, epi_v),
                            (_OFF_G, 2 * D_MODEL, epi_g), (_OFF_Z, SSD_D_INNER, epi_z)):
        stages += [(off + c, COL_CHUNK, epi, c) for c in range(0, total, COL_CHUNK)]
    stages.append((_OFF_DT, LANES, epi_dt, 0))

    def mm_block(s, rb):
        off, width, _, _ = stages[s]
        r0 = rb * ROW_BLOCK
        u_ref[s % 2, SUBLANES + r0:SUBLANES + r0 + ROW_BLOCK, :width] = jnp.dot(
            hb_ref[r0:r0 + ROW_BLOCK, :], w_ref[:, off:off + width], preferred_element_type=F32)

    def load_halo(s):
        _, _, epi, c = stages[s]
        if epi is epi_xbc:
            u_ref[s % 2, 0:SUBLANES, :] = jnp.where(first, 0.0, carry_ref[:, c:c + COL_CHUNK])

    def save_halo(s):
        _, _, epi, c = stages[s]
        if epi is epi_xbc:
            carry_ref[:, c:c + COL_CHUNK] = u_ref[s % 2, rows:rows + SUBLANES, :]

    n_rb = rows // ROW_BLOCK
    load_halo(0)
    for rb in range(n_rb):
        mm_block(0, rb)
    for s in range(len(stages)):
        if s + 1 < len(stages):
            load_halo(s + 1)
        for rb in range(n_rb):
            if s + 1 < len(stages):
                mm_block(s + 1, rb)
            stages[s][2](s % 2, rb * ROW_BLOCK, stages[s][3])
        save_halo(s)


def _rope_tables(seq):
    pos = jnp.arange(seq, dtype=F32)
    inv_freq = jnp.power(ROPE_THETA, -jnp.arange(0, ROPE_DIM, 2, dtype=F32) / ROPE_DIM)
    ang = pos[:, None] * inv_freq[None, :]
    cos, sin = jnp.cos(ang), jnp.sin(ang)
    half = ROPE_DIM // 2
    d = jnp.arange(LANES) % DA_HEAD_DIM
    idx = d % half
    cosf = jnp.where(d[None, :] < ROPE_DIM, cos[:, idx], 1.0)
    sina = jnp.where(d[None, :] < half, -sin[:, idx], 0.0)
    sinb = jnp.where((d[None, :] >= half) & (d[None, :] < ROPE_DIM), sin[:, idx], 0.0)
    return cosf.astype(F32), sina.astype(F32), sinb.astype(F32)


def _inproj(x2, mod, pre_norm_w, w_cat, conv_w, conv_b, dt_bias_pad, seq):
    tokens, d = x2.shape
    tiles_per_seq = seq // ROW_TILE
    cosf, sina, sinb = _rope_tables(seq)

    def row_spec(width):
        return pl.BlockSpec((ROW_TILE, width), lambda i: (i, 0))

    def mod_spec(which):
        return pl.BlockSpec((None, None, 1, d), lambda i: (which, i // tiles_per_seq, 0, 0))

    def rope_spec():
        return pl.BlockSpec((ROW_TILE, LANES), lambda i: (i % tiles_per_seq, 0))

    def transposed_spec():
        return pl.BlockSpec((DA_WIDTH, ROW_TILE), lambda i: (0, i))

    transposed = jax.ShapeDtypeStruct((DA_WIDTH, tokens), BF16)
    out_shapes = (
        jax.ShapeDtypeStruct((tokens, SSD_D_INNER), BF16),
        jax.ShapeDtypeStruct((tokens, SSD_CONV_DIM), BF16),
        transposed,
        jax.ShapeDtypeStruct((tokens, DA_WIDTH), BF16),
        transposed,
        jax.ShapeDtypeStruct((tokens, 2 * D_MODEL), BF16),
        jax.ShapeDtypeStruct((tokens, LANES), F32),
    )
    return pl.pallas_call(
        functools.partial(_inproj_kernel, tiles_per_seq),
        grid=(tokens // ROW_TILE,),
        in_specs=[
            row_spec(d), mod_spec(1), mod_spec(0), _resident((1, d)), _resident((d, _W_COLS)),
            rope_spec(), rope_spec(), rope_spec(),
            _resident((SSD_CONV, SSD_CONV_DIM)), _resident((1, SSD_CONV_DIM)), _resident((1, LANES)),
        ],
        out_specs=[transposed_spec() if s is transposed else row_spec(s.shape[1])
                   for s in out_shapes],
        out_shape=out_shapes,
        scratch_shapes=[
            pltpu.VMEM((SUBLANES, SSD_CONV_DIM), F32),
            pltpu.VMEM((ROW_TILE, d), BF16),
            pltpu.VMEM((2, SUBLANES + ROW_TILE, COL_CHUNK), F32),
        ],
        compiler_params=pltpu.CompilerParams(dimension_semantics=("arbitrary",),
                                             vmem_limit_bytes=VMEM_LIMIT_BYTES),
        name="inproj",
    )(x2, mod, mod, pre_norm_w, w_cat, cosf, sina, sinb, conv_w, conv_b, dt_bias_pad)


def _split3(a):
    hi = a.astype(BF16)
    r = a - hi.astype(F32)
    mid = r.astype(BF16)
    lo = (r - mid.astype(F32)).astype(BF16)
    return hi, mid, lo


def _ssd_kernel(xbc_ref, dt_ref, z_ref, a_ref, dskip_ref, nw_ref, expand_ref, o_ref, state_ref):
    l = SSD_CHUNK
    n = SSD_D_STATE
    gw = SSD_D_INNER // SSD_GROUPS
    heads_per_group = SSD_HEADS // SSD_GROUPS

    @pl.when(pl.program_id(1) == 0)
    def _():
        state_ref[...] = jnp.zeros_like(state_ref)

    is_head = lax.broadcasted_iota(jnp.int32, (1, LANES), 1) < SSD_HEADS
    a_scaled = jnp.where(is_head, -jnp.exp(a_ref[...]) * math.log2(math.e), 0.0)
    row = lax.broadcasted_iota(jnp.int32, (l, l), 0)
    col = lax.broadcasted_iota(jnp.int32, (l, l), 1)
    causal = row >= col
    tril = jnp.where(causal, 1.0, 0.0).astype(BF16)
    lane = lax.broadcasted_iota(jnp.int32, (l, LANES), 1)

    def program(s):
        xbc = xbc_ref[s]
        xs = xbc[:, :SSD_D_INNER].astype(F32)
        dt = dt_ref[s]
        la = dt * a_scaled
        expand = expand_ref[...]
        acum = sum(jnp.dot(tril, part, preferred_element_type=F32) for part in _split3(la))
        yield
        acum_t = acum.T
        alast = acum[l - 1:l, :]
        dte = jnp.exp2(alast - acum)
        dt_x = jnp.dot(dt.astype(BF16), expand, preferred_element_type=F32)
        yield
        w_x = jnp.dot((dt * dte).astype(BF16), expand, preferred_element_type=F32)
        cdec = jnp.broadcast_to(jnp.exp2(alast), (SUBLANES, LANES))
        cdec_x = sum(jnp.dot(part, expand, preferred_element_type=F32) for part in _split3(cdec))[0:1]
        yield
        xd = (xs * dt_x).astype(BF16)
        xw = (xs * w_x).astype(BF16)
        yield
        pairs = []
        for g in range(SSD_GROUPS):
            bm = xbc[:, SSD_D_INNER + g * n:SSD_D_INNER + (g + 1) * n]
            cm = xbc[:, SSD_D_INNER + (SSD_GROUPS + g) * n:SSD_D_INNER + (SSD_GROUPS + g + 1) * n]
            cb = lax.dot_general(cm, bm, (((1,), (1,)), ((), ())), preferred_element_type=F32)
            cmf = cm.astype(F32)
            state = state_ref[s, g]
            state_b = state.astype(BF16)
            for pr in range(heads_per_group // 2):
                res = []
                for sub in range(2):
                    h = g * heads_per_group + 2 * pr + sub
                    a_col = jnp.broadcast_to(acum[:, h:h + 1], (l, l))
                    seg = a_col - acum_t[h:h + 1, :]
                    decay = jnp.exp2(jnp.where(causal, seg, -jnp.inf))
                    lhs = jnp.concatenate([(cb * decay).astype(BF16),
                                           (cmf * jnp.exp2(a_col)).astype(BF16)], axis=1)
                    c0 = g * gw + pr * LANES
                    rhs = jnp.concatenate([xd[:, c0:c0 + LANES],
                                           state_b[:, pr * LANES:(pr + 1) * LANES]], axis=0)
                    res.append(jnp.dot(lhs, rhs, preferred_element_type=F32))
                    yield
                pairs.append(jnp.where(lane < SSD_HEAD_DIM, res[0], res[1]))
            bm_t = bm.astype(F32).T.astype(BF16)
            new = jnp.dot(bm_t, xw[:, g * gw:(g + 1) * gw], preferred_element_type=F32)
            state_ref[s, g] = state * cdec_x[:, g * gw:(g + 1) * gw] + new
            yield
        y = jnp.concatenate(pairs, axis=1)
        y = y + dskip_ref[...] * xs
        y = y * _silu(z_ref[s].astype(F32))
        yield
        y = jnp.concatenate([_rms(y[:, g * gw:(g + 1) * gw], SSD_NORM_EPS)
                             for g in range(SSD_GROUPS)], axis=1)
        o_ref[s] = (y * nw_ref[...]).astype(BF16)

    programs = [program(s) for s in range(SSD_BATCH)]
    while programs:
        for p in list(programs):
            if next(p, programs) is programs:
                programs.remove(p)


def _ssd(xbc, dt, z, a_row, dskip_x, norm_w, bsz, seq):
    nc = seq // SSD_CHUNK
    head = jnp.arange(LANES)[:, None]
    chan = jnp.arange(SSD_D_INNER)[None, :] // SSD_HEAD_DIM
    expand = (head == chan).astype(BF16)

    def seq_spec(width):
        return pl.BlockSpec((SSD_BATCH, SSD_CHUNK, width), lambda b, c: (b, c, 0))

    out = pl.pallas_call(
        _ssd_kernel,
        grid=(bsz // SSD_BATCH, nc),
        in_specs=[
            seq_spec(SSD_CONV_DIM), seq_spec(LANES), seq_spec(SSD_D_INNER),
            _resident((1, LANES)), _resident((1, SSD_D_INNER)), _resident((1, SSD_D_INNER)),
            _resident((LANES, SSD_D_INNER)),
        ],
        out_specs=seq_spec(SSD_D_INNER),
        out_shape=jax.ShapeDtypeStruct((bsz, seq, SSD_D_INNER), BF16),
        scratch_shapes=[pltpu.VMEM((SSD_BATCH, SSD_GROUPS, SSD_D_STATE, SSD_D_INNER // SSD_GROUPS),
                                   F32)],
        compiler_params=pltpu.CompilerParams(dimension_semantics=("arbitrary", "arbitrary"),
                                             vmem_limit_bytes=VMEM_LIMIT_BYTES),
        name="ssd",
    )(xbc.reshape(bsz, seq, -1), dt.reshape(bsz, seq, -1), z.reshape(bsz, seq, -1),
      a_row, dskip_x, norm_w, expand)
    return out.reshape(bsz * seq, SSD_D_INNER)


def _attn_kernel(lq1_ref, lk1_ref, lq2_ref, lk2_ref, sw_ref, qt_ref, k_ref, vt_ref, o_ref,
                 st_ref, p_ref):
    hw = 2 * DA_HEAD_DIM
    seq = qt_ref.shape[1]
    t = ATT_BLOCK
    nq = seq // t
    lam = (jnp.exp(jnp.sum(lq1_ref[...] * lk1_ref[...], axis=-1, keepdims=True))
           - jnp.exp(jnp.sum(lq2_ref[...] * lk2_ref[...], axis=-1, keepdims=True)) + LAM_INIT)
    dim = lax.broadcasted_iota(jnp.int32, (hw, t), 0)
    query = lax.broadcasted_iota(jnp.int32, (t, 2 * t), 1)
    query = jnp.where(query >= t, query - t, query)
    key_le_query = lax.broadcasted_iota(jnp.int32, (t, 2 * t), 0) <= query

    def program(hd):
        hs = slice(hd * hw, (hd + 1) * hw)

        def masked_q(i):
            qt = qt_ref[hs, i * t:(i + 1) * t]
            zero = jnp.zeros_like(qt)
            return jnp.concatenate([jnp.where(dim < DA_HEAD_DIM, qt, zero),
                                    jnp.where(dim >= DA_HEAD_DIM, qt, zero)], axis=1)

        def score_piece(i, j, qm):
            st = jnp.dot(k_ref[j * t:(j + 1) * t, hs], qm, preferred_element_type=F32)
            if j == i:
                st = jnp.where(key_le_query, st, -jnp.inf)
            st_ref[hd, i % 2, j * t:(j + 1) * t, :] = st
            return jnp.max(st.reshape(t // SUBLANES, SUBLANES, 2 * t), axis=0)

        def softmax_piece(i, j, m):
            e = jnp.exp2(st_ref[hd, i % 2, j * t:(j + 1) * t, :] - m)
            p_ref[hd, i % 2, j * t:(j + 1) * t, :] = e.astype(BF16)
            return jnp.sum(e.reshape(t // SUBLANES, SUBLANES, 2 * t), axis=0)

        def pv_piece(i, j):
            return jnp.dot(vt_ref[hs, j * t:(j + 1) * t], p_ref[hd, i % 2, j * t:(j + 1) * t, :],
                           preferred_element_type=F32)

        def finish(i, acc, l):
            acc = acc / l
            d = acc[:, :t] - lam * acc[:, t:]
            d = d * lax.rsqrt(jnp.mean(d * d, axis=0, keepdims=True) + SUBLN_EPS)
            y = (d.T * sw_ref[...]) * (1.0 - LAM_INIT)
            o_ref[i * t:(i + 1) * t, hs] = y.astype(BF16)

        m = jnp.max(score_piece(0, 0, masked_q(0)), axis=0, keepdims=True)
        l_prev = None
        for i in range(nq + 1):
            n_score = i + 2 if i + 1 < nq else 0
            n_soft = i + 1 if i < nq else 0
            n_pv = i if i >= 1 else 0
            qm_next = masked_q(i + 1) if n_score else None
            m_next = None
            acc = None
            l = jnp.zeros((SUBLANES, 2 * t), F32)
            for j in range(max(n_score, n_soft, n_pv)):
                if j < n_score:
                    mj = score_piece(i + 1, j, qm_next)
                    m_next = mj if m_next is None else jnp.maximum(m_next, mj)
                if j < n_pv:
                    part = pv_piece(i - 1, j)
                    acc = part if acc is None else acc + part
                if j < n_soft:
                    l = l + softmax_piece(i, j, m)
                yield
            if n_pv:
                finish(i - 1, acc, l_prev)
            l_prev = jnp.sum(l, axis=0, keepdims=True)
            m = None if m_next is None else jnp.max(m_next, axis=0, keepdims=True)

    programs = [program(hd) for hd in range(ATT_HEADS)]
    while programs:
        for p in list(programs):
            if next(p, programs) is programs:
                programs.remove(p)


def _attn(qt, k, vt, lq1, lk1, lq2, lk2, subln_w, bsz, seq):
    tokens = k.shape[0]
    hw = ATT_HEADS * 2 * DA_HEAD_DIM
    lam_spec = _resident((1, DA_HEAD_DIM))
    transposed_spec = pl.BlockSpec((hw, seq), lambda b, h: (h, b))
    return pl.pallas_call(
        _attn_kernel,
        grid=(bsz, DA_HEADS // ATT_HEADS),
        in_specs=[
            lam_spec, lam_spec, lam_spec, lam_spec, _resident((1, 2 * DA_HEAD_DIM)),
            transposed_spec,
            pl.BlockSpec((seq, hw), lambda b, h: (b, h)),
            transposed_spec,
        ],
        out_specs=pl.BlockSpec((seq, hw), lambda b, h: (b, h)),
        out_shape=jax.ShapeDtypeStruct((tokens, DA_WIDTH), BF16),
        scratch_shapes=[pltpu.VMEM((ATT_HEADS, 2, seq, 2 * ATT_BLOCK), F32),
                        pltpu.VMEM((ATT_HEADS, 2, seq, 2 * ATT_BLOCK), BF16)],
        compiler_params=pltpu.CompilerParams(dimension_semantics=("arbitrary", "arbitrary"),
                                             vmem_limit_bytes=VMEM_LIMIT_BYTES),
        name="attn",
    )(lq1, lk1, lq2, lk2, subln_w, qt, k, vt)


def _merge_kernel(x_ref, ys_ref, ya_ref, g_ref, gate_ref, nw_ref, wso_ref, wao_ref, wout_ref, o_ref):
    d = x_ref.shape[1]
    so = jnp.dot(ys_ref[...], wso_ref[...], preferred_element_type=F32)
    ao = jnp.dot(ya_ref[...], wao_ref[...], preferred_element_type=F32)
    merged = g_ref[:, :d].astype(F32) * so + g_ref[:, d:].astype(F32) * ao
    mix = jnp.dot(merged.astype(BF16), wout_ref[...], preferred_element_type=F32)
    o_ref[...] = x_ref[...] + gate_ref[...] * (_rms(mix, NORM_EPS) * nw_ref[...])


def _merge(x2, y_ssd, y_att, gates, mod, post_norm_w, w_ssd_o, w_attn_o, w_out, seq):
    tokens, d = x2.shape
    tiles_per_seq = seq // ROW_TILE

    def row_spec(width):
        return pl.BlockSpec((ROW_TILE, width), lambda i: (i, 0))

    return pl.pallas_call(
        _merge_kernel,
        grid=(tokens // ROW_TILE,),
        in_specs=[
            row_spec(d), row_spec(d), row_spec(d), row_spec(2 * d),
            pl.BlockSpec((None, None, 1, d), lambda i: (2, i // tiles_per_seq, 0, 0)),
            _resident((1, d)), _resident((d, d)), _resident((d, d)), _resident((d, d)),
        ],
        out_specs=row_spec(d),
        out_shape=jax.ShapeDtypeStruct((tokens, d), F32),
        compiler_params=pltpu.CompilerParams(dimension_semantics=("arbitrary",),
                                             vmem_limit_bytes=VMEM_LIMIT_BYTES),
        name="merge",
    )(x2, y_ssd, y_att, gates, mod, post_norm_w, w_ssd_o, w_attn_o, w_out)


def _ffn_kernel(tiles_per_seq, x_ref, sc_ref, sh_ref, gate_ref, nw_ref, pw_ref, wup_ref, cw_ref,
                cb_ref, wdn_ref, o_ref, carry_ref, acc_ref, hb_ref, u_ref, act_ref):
    rows = x_ref.shape[0]
    width = 2 * FF_CHUNK
    n_chunks = D_FF // FF_CHUNK
    first = (pl.program_id(0) % tiles_per_seq) == 0
    x = x_ref[...]
    h = (_rms(x, NORM_EPS) * nw_ref[...]) * (1.0 + sc_ref[...]) + sh_ref[...]
    hb_ref[...] = h.astype(BF16)
    k0 = math.sqrt(2.0 / math.pi)

    def up_block(chunk, rb):
        r0 = rb * ROW_BLOCK
        u_ref[chunk % 2, SUBLANES + r0:SUBLANES + r0 + ROW_BLOCK, :] = jnp.dot(
            hb_ref[r0:r0 + ROW_BLOCK, :], wup_ref[:, chunk * width:(chunk + 1) * width],
            preferred_element_type=F32)

    def gate_block(chunk, rb):
        r0 = rb * ROW_BLOCK
        slot = chunk % 2
        w = cw_ref[:, chunk * width:(chunk + 1) * width]
        y = cb_ref[:, chunk * width:(chunk + 1) * width]
        for k in range(FFN_CONV):
            lo = SUBLANES + r0 - k
            y = y + u_ref[slot, lo:lo + ROW_BLOCK, :] * w[FFN_CONV - 1 - k:FFN_CONV - k]
        ug = y[:, :FF_CHUNK]
        uv = y[:, FF_CHUNK:]
        act = 0.5 * ug * (1.0 + jnp.tanh(k0 * (ug + 0.044715 * (ug * ug * ug)))) * uv
        act_ref[slot, r0:r0 + ROW_BLOCK, :] = act.astype(BF16)

    def down_block(chunk, rb):
        r0 = rb * ROW_BLOCK
        part = jnp.dot(act_ref[chunk % 2, r0:r0 + ROW_BLOCK, :],
                       wdn_ref[chunk * FF_CHUNK:(chunk + 1) * FF_CHUNK, :],
                       preferred_element_type=F32)
        if chunk == 0:
            acc_ref[r0:r0 + ROW_BLOCK, :] = part
        else:
            acc_ref[r0:r0 + ROW_BLOCK, :] += part

    def load_halo(chunk):
        u_ref[chunk % 2, 0:SUBLANES, :] = jnp.where(
            first, 0.0, carry_ref[:, chunk * width:(chunk + 1) * width])

    def save_halo(chunk):
        carry_ref[:, chunk * width:(chunk + 1) * width] = u_ref[chunk % 2, rows:rows + SUBLANES, :]

    n_rb = rows // ROW_BLOCK
    load_halo(0)
    for rb in range(n_rb):
        up_block(0, rb)
    for chunk in range(n_chunks + 1):
        if chunk + 1 < n_chunks:
            load_halo(chunk + 1)
        for rb in range(n_rb):
            if chunk + 1 < n_chunks:
                up_block(chunk + 1, rb)
            if chunk >= 1:
                down_block(chunk - 1, rb)
            if chunk < n_chunks:
                gate_block(chunk, rb)
        if chunk < n_chunks:
            save_halo(chunk)

    o_ref[...] = x + gate_ref[...] * (_rms(acc_ref[...], NORM_EPS) * pw_ref[...])


def _chunk_major(a):
    lead = a.shape[:-1]
    g = a[..., :D_FF].reshape(*lead, D_FF // FF_CHUNK, FF_CHUNK)
    v = a[..., D_FF:].reshape(*lead, D_FF // FF_CHUNK, FF_CHUNK)
    return jnp.concatenate([g, v], axis=-1).reshape(*lead, 2 * D_FF)


def _ffn(x1, mod, pre_norm_w, post_norm_w, w_up, conv_w, conv_b, w_down, seq):
    tokens, d = x1.shape
    rows = FFN_ROW_TILE
    tiles_per_seq = seq // rows

    def row_spec(width):
        return pl.BlockSpec((rows, width), lambda i: (i, 0))

    def mod_spec(which):
        return pl.BlockSpec((None, None, 1, d), lambda i: (which, i // tiles_per_seq, 0, 0))

    return pl.pallas_call(
        functools.partial(_ffn_kernel, tiles_per_seq),
        grid=(tokens // rows,),
        in_specs=[
            row_spec(d), mod_spec(4), mod_spec(3), mod_spec(5), _resident((1, d)), _resident((1, d)),
            _resident((d, 2 * D_FF)), _resident((FFN_CONV, 2 * D_FF)), _resident((1, 2 * D_FF)),
            _resident((D_FF, d)),
        ],
        out_specs=row_spec(d),
        out_shape=jax.ShapeDtypeStruct((tokens, d), F32),
        scratch_shapes=[
            pltpu.VMEM((SUBLANES, 2 * D_FF), F32),
            pltpu.VMEM((rows, d), F32),
            pltpu.VMEM((rows, d), BF16),
            pltpu.VMEM((2, SUBLANES + rows, 2 * FF_CHUNK), F32),
            pltpu.VMEM((2, rows, FF_CHUNK), BF16),
        ],
        compiler_params=pltpu.CompilerParams(dimension_semantics=("arbitrary",),
                                             vmem_limit_bytes=VMEM_LIMIT_BYTES),
        name="ffn",
    )(x1, mod, mod, mod, pre_norm_w, post_norm_w, _chunk_major(w_up), _chunk_major(conv_w),
      _chunk_major(conv_b), w_down)


def kernel(x, c, w_ada, b_ada, pre_norm1_w, w_in, conv_ssd_w, conv_ssd_b, dt_bias, a_log, d_skip,
           ssd_norm_w, w_ssd_o, lambda_q1, lambda_k1, lambda_q2, lambda_k2, subln_w, w_attn_o, w_out,
           post_norm1_w, pre_norm2_w, w_up, conv_ffn_w, conv_ffn_b, w_down, post_norm2_w):
    bsz, seq, d = x.shape
    assert d == D_MODEL and seq % ROW_TILE == 0 and seq % ATT_BLOCK == 0 and seq % SSD_CHUNK == 0
    assert seq % FFN_ROW_TILE == 0 and bsz % SSD_BATCH == 0
    assert w_ada.shape[0] == 1, "single-layer block"
    layer = 0
    x2 = x.reshape(bsz * seq, d)

    mod = _ada(c, w_ada[layer], b_ada[layer]).reshape(6, bsz, 1, d)

    w = w_in[layer].astype(BF16)
    s0 = SSD_D_INNER
    s1 = s0 + SSD_CONV_DIM
    s2 = s1 + SSD_HEADS
    s3 = s2 + DA_WIDTH
    s4 = s3 + DA_WIDTH
    s5 = s4 + DA_WIDTH
    w_cat = jnp.concatenate(
        [w[:, :s0], w[:, s0:s1], w[:, s2:s3], w[:, s3:s4], w[:, s4:s5], w[:, s5:],
         jnp.pad(w[:, s1:s2], ((0, 0), (0, LANES - SSD_HEADS)))], axis=1)
    dt_bias_pad = jnp.pad(dt_bias[layer], (0, LANES - SSD_HEADS)).reshape(1, LANES)

    z, xbc, qt, k, vt, gates, dt = _inproj(
        x2, mod, pre_norm1_w[layer].reshape(1, d), w_cat, conv_ssd_w[layer],
        conv_ssd_b[layer].reshape(1, -1), dt_bias_pad, seq)

    a_row = jnp.pad(a_log[layer].astype(F32), (0, LANES - SSD_HEADS)).reshape(1, LANES)
    dskip_x = jnp.repeat(d_skip[layer].astype(F32), SSD_HEAD_DIM).reshape(1, SSD_D_INNER)
    y_ssd = _ssd(xbc, dt, z, a_row, dskip_x, ssd_norm_w[layer].reshape(1, -1), bsz, seq)

    y_att = _attn(qt, k, vt, lambda_q1[layer].reshape(1, -1), lambda_k1[layer].reshape(1, -1),
                  lambda_q2[layer].reshape(1, -1), lambda_k2[layer].reshape(1, -1),
                  subln_w[layer].reshape(1, -1), bsz, seq)

    x1 = _merge(x2, y_ssd, y_att, gates, mod, post_norm1_w[layer].reshape(1, d),
                w_ssd_o[layer].astype(BF16), w_attn_o[layer].astype(BF16), w_out[layer].astype(BF16),
                seq)

    out = _ffn(x1, mod, pre_norm2_w[layer].reshape(1, d), post_norm2_w[layer].reshape(1, d),
               w_up[layer].astype(BF16), conv_ffn_w[layer], conv_ffn_b[layer].reshape(1, -1),
               w_down[layer].astype(BF16), seq)
    return out.reshape(bsz, seq, d)
```

```python
import functools
import math

import jax
import jax.numpy as jnp
from jax import lax
from jax.experimental import pallas as pl
from jax.experimental.pallas import tpu as pltpu

F32 = jnp.float32
BF16 = jnp.bfloat16

D_MODEL = 1024
SSD_D_INNER = 1024
SSD_HEAD_DIM = 64
SSD_HEADS = SSD_D_INNER // SSD_HEAD_DIM
SSD_GROUPS = 2
SSD_D_STATE = 128
SSD_CONV = 4
SSD_CHUNK = 128
SSD_CONV_DIM = SSD_D_INNER + 2 * SSD_GROUPS * SSD_D_STATE
SSD_NORM_EPS = 1e-5
DA_HEADS = 8
DA_HEAD_DIM = 64
DA_WIDTH = DA_HEADS * 2 * DA_HEAD_DIM
ROPE_THETA = 500000.0
ROPE_DIM = DA_HEAD_DIM // 4
SUBLN_EPS = 1e-5
D_FF = 2816
FFN_CONV = 3
NORM_EPS = 1e-6
LAM_INIT = 0.8 - 0.6 * math.exp(-0.3 * 0)

LANES = 128
SUBLANES = 8
VMEM_LIMIT_BYTES = 56 * 1024 * 1024

ROW_TILE = 512
FFN_ROW_TILE = 512
COL_CHUNK = 512
FF_CHUNK = 256
ATT_BLOCK = 256
ROW_BLOCK = 128
SSD_BATCH = 2
ATT_HEADS = 2


def _silu(x):
    return x * jax.nn.sigmoid(x)


def _rms(x, eps):
    return x * lax.rsqrt(jnp.mean(x * x, axis=-1, keepdims=True) + eps)


def _resident(shape):
    nd = len(shape)
    return pl.BlockSpec(shape, lambda *_: (0,) * nd, pipeline_mode=pl.Buffered(1))


def _ada_kernel(c_ref, w_ref, b_ref, o_ref):
    c = c_ref[...]
    o_ref[0] = jnp.dot(_silu(c), w_ref[...], preferred_element_type=F32,
                       precision=lax.Precision.HIGHEST) + b_ref[0]


def _ada(c, w_ada, b_ada):
    bsz, d = c.shape
    n_mod = w_ada.shape[1] // d
    return pl.pallas_call(
        _ada_kernel,
        grid=(n_mod,),
        in_specs=[
            pl.BlockSpec((bsz, d), lambda j: (0, 0)),
            pl.BlockSpec((d, d), lambda j: (0, j)),
            pl.BlockSpec((1, 1, d), lambda j: (j, 0, 0)),
        ],
        out_specs=pl.BlockSpec((1, bsz, d), lambda j: (j, 0, 0)),
        out_shape=jax.ShapeDtypeStruct((n_mod, bsz, d), F32),
        compiler_params=pltpu.CompilerParams(dimension_semantics=("arbitrary",),
                                             vmem_limit_bytes=VMEM_LIMIT_BYTES),
        name="ada",
    )(c, w_ada, b_ada.reshape(n_mod, 1, d))


def _inproj_kernel(tiles_per_seq, x_ref, sc_ref, sh_ref, nw_ref, wssd_ref, watt_ref, wdt_ref,
                   cosf_ref, sina_ref, sinb_ref, cw_ref, cb_ref, dtb_ref,
                   z_ref, xbc_ref, qt_ref, k_ref, vt_ref, g_ref, dt_ref, carry_ref, hb_ref, u_ref):
    rows = x_ref.shape[0]
    first = (pl.program_id(0) % tiles_per_seq) == 0
    x = x_ref[...]
    h = (_rms(x, NORM_EPS) * nw_ref[...]) * (1.0 + sc_ref[...]) + sh_ref[...]
    hb_ref[...] = h.astype(BF16)
    half = ROPE_DIM // 2
    q_scale = DA_HEAD_DIM ** -0.5 * math.log2(math.e)

    def rope(t, r0):
        rs = slice(r0, r0 + ROW_BLOCK)
        return (t * cosf_ref[rs, :] + pltpu.roll(t, LANES - half, axis=1) * sina_ref[rs, :]
                + pltpu.roll(t, half, axis=1) * sinb_ref[rs, :])

    def block(slot, r0, width=COL_CHUNK):
        return u_ref[slot, SUBLANES + r0:SUBLANES + r0 + ROW_BLOCK, :width]

    def epi_z(slot, r0, c):
        z_ref[r0:r0 + ROW_BLOCK, c:c + COL_CHUNK] = block(slot, r0).astype(BF16)

    def epi_xbc(slot, r0, c):
        w = cw_ref[:, c:c + COL_CHUNK]
        y = cb_ref[:, c:c + COL_CHUNK]
        for k in range(SSD_CONV):
            lo = SUBLANES + r0 - k
            y = y + u_ref[slot, lo:lo + ROW_BLOCK, :] * w[SSD_CONV - 1 - k:SSD_CONV - k]
        xbc_ref[r0:r0 + ROW_BLOCK, c:c + COL_CHUNK] = _silu(y).astype(BF16)

    def epi_q(slot, r0, c):
        blk = block(slot, r0)
        for s in range(0, COL_CHUNK, LANES):
            r = rope(blk[:, s:s + LANES], r0) * q_scale
            qt_ref[c + s:c + s + LANES, r0:r0 + ROW_BLOCK] = r.T.astype(BF16)

    def epi_k(slot, r0, c):
        blk = block(slot, r0)
        for s in range(0, COL_CHUNK, LANES):
            k_ref[r0:r0 + ROW_BLOCK, c + s:c + s + LANES] = rope(blk[:, s:s + LANES], r0).astype(BF16)

    def epi_v(slot, r0, c):
        vt_ref[c:c + COL_CHUNK, r0:r0 + ROW_BLOCK] = block(slot, r0).T.astype(BF16)

    def epi_g(slot, r0, c):
        g_ref[r0:r0 + ROW_BLOCK, c:c + COL_CHUNK] = jax.nn.sigmoid(block(slot, r0)).astype(BF16)

    def epi_dt(slot, r0, c):
        dt_raw = block(slot, r0, LANES) + dtb_ref[...]
        dt_ref[r0:r0 + ROW_BLOCK, :] = (jnp.maximum(dt_raw, 0.0)
                                        + jnp.log1p(jnp.exp(-jnp.abs(dt_raw))))

    stages = []
    for w_ref, off, total, epi in (
            (wssd_ref, SSD_D_INNER, SSD_CONV_DIM, epi_xbc), (watt_ref, 0, DA_WIDTH, epi_q),
            (watt_ref, DA_WIDTH, DA_WIDTH, epi_k), (watt_ref, 2 * DA_WIDTH, DA_WIDTH, epi_v),
            (watt_ref, 3 * DA_WIDTH, 2 * D_MODEL, epi_g), (wssd_ref, 0, SSD_D_INNER, epi_z)):
        stages += [(w_ref, off + c, COL_CHUNK, epi, c) for c in range(0, total, COL_CHUNK)]
    stages.append((wdt_ref, 0, LANES, epi_dt, 0))

    def mm_block(s, rb):
        w_ref, off, width, _, _ = stages[s]
        r0 = rb * ROW_BLOCK
        u_ref[s % 2, SUBLANES + r0:SUBLANES + r0 + ROW_BLOCK, :width] = jnp.dot(
            hb_ref[r0:r0 + ROW_BLOCK, :], w_ref[:, off:off + width], preferred_element_type=F32)

    def load_halo(s):
        _, _, _, epi, c = stages[s]
        if epi is epi_xbc:
            u_ref[s % 2, 0:SUBLANES, :] = jnp.where(first, 0.0, carry_ref[:, c:c + COL_CHUNK])

    def save_halo(s):
        _, _, _, epi, c = stages[s]
        if epi is epi_xbc:
            carry_ref[:, c:c + COL_CHUNK] = u_ref[s % 2, rows:rows + SUBLANES, :]

    n_rb = rows // ROW_BLOCK
    load_halo(0)
    for rb in range(n_rb):
        mm_block(0, rb)
    for s in range(len(stages)):
        if s + 1 < len(stages):
            load_halo(s + 1)
        for rb in range(n_rb):
            if s + 1 < len(stages):
                mm_block(s + 1, rb)
            stages[s][3](s % 2, rb * ROW_BLOCK, stages[s][4])
        save_halo(s)


def _rope_tables(seq):
    pos = jnp.arange(seq, dtype=F32)
    inv_freq = jnp.power(ROPE_THETA, -jnp.arange(0, ROPE_DIM, 2, dtype=F32) / ROPE_DIM)
    ang = pos[:, None] * inv_freq[None, :]
    cos, sin = jnp.cos(ang), jnp.sin(ang)
    half = ROPE_DIM // 2
    d = jnp.arange(LANES) % DA_HEAD_DIM
    idx = d % half
    cosf = jnp.where(d[None, :] < ROPE_DIM, cos[:, idx], 1.0)
    sina = jnp.where(d[None, :] < half, -sin[:, idx], 0.0)
    sinb = jnp.where((d[None, :] >= half) & (d[None, :] < ROPE_DIM), sin[:, idx], 0.0)
    return cosf.astype(F32), sina.astype(F32), sinb.astype(F32)


def _inproj(x2, mod, pre_norm_w, w_ssd, w_att, w_dt, conv_w, conv_b, dt_bias_pad, seq):
    tokens, d = x2.shape
    tiles_per_seq = seq // ROW_TILE
    cosf, sina, sinb = _rope_tables(seq)

    def row_spec(width):
        return pl.BlockSpec((ROW_TILE, width), lambda i: (i, 0))

    def mod_spec(which):
        return pl.BlockSpec((None, None, 1, d), lambda i: (which, i // tiles_per_seq, 0, 0))

    def rope_spec():
        return pl.BlockSpec((ROW_TILE, LANES), lambda i: (i % tiles_per_seq, 0))

    def transposed_spec():
        return pl.BlockSpec((DA_WIDTH, ROW_TILE), lambda i: (0, i))

    transposed = jax.ShapeDtypeStruct((DA_WIDTH, tokens), BF16)
    out_shapes = (
        jax.ShapeDtypeStruct((tokens, SSD_D_INNER), BF16),
        jax.ShapeDtypeStruct((tokens, SSD_CONV_DIM), BF16),
        transposed,
        jax.ShapeDtypeStruct((tokens, DA_WIDTH), BF16),
        transposed,
        jax.ShapeDtypeStruct((tokens, 2 * D_MODEL), BF16),
        jax.ShapeDtypeStruct((tokens, LANES), F32),
    )
    return pl.pallas_call(
        functools.partial(_inproj_kernel, tiles_per_seq),
        grid=(tokens // ROW_TILE,),
        in_specs=[
            row_spec(d), mod_spec(1), mod_spec(0), _resident((1, d)),
            _resident(w_ssd.shape), _resident(w_att.shape), _resident(w_dt.shape),
            rope_spec(), rope_spec(), rope_spec(),
            _resident((SSD_CONV, SSD_CONV_DIM)), _resident((1, SSD_CONV_DIM)), _resident((1, LANES)),
        ],
        out_specs=[transposed_spec() if s is transposed else row_spec(s.shape[1])
                   for s in out_shapes],
        out_shape=out_shapes,
        scratch_shapes=[
            pltpu.VMEM((SUBLANES, SSD_CONV_DIM), F32),
            pltpu.VMEM((ROW_TILE, d), BF16),
            pltpu.VMEM((2, SUBLANES + ROW_TILE, COL_CHUNK), F32),
        ],
        compiler_params=pltpu.CompilerParams(dimension_semantics=("arbitrary",),
                                             vmem_limit_bytes=VMEM_LIMIT_BYTES),
        name="inproj",
    )(x2, mod, mod, pre_norm_w, w_ssd, w_att, w_dt, cosf, sina, sinb, conv_w, conv_b, dt_bias_pad)


def _split3(a):
    hi = a.astype(BF16)
    r = a - hi.astype(F32)
    mid = r.astype(BF16)
    lo = (r - mid.astype(F32)).astype(BF16)
    return hi, mid, lo


def _ssd_kernel(xbc_ref, dt_ref, z_ref, a_ref, dskip_ref, nw_ref, expand_ref, o_ref, state_ref):
    l = SSD_CHUNK
    n = SSD_D_STATE
    gw = SSD_D_INNER // SSD_GROUPS
    heads_per_group = SSD_HEADS // SSD_GROUPS

    @pl.when(pl.program_id(1) == 0)
    def _():
        state_ref[...] = jnp.zeros_like(state_ref)

    is_head = lax.broadcasted_iota(jnp.int32, (1, LANES), 1) < SSD_HEADS
    a_scaled = jnp.where(is_head, -jnp.exp(a_ref[...]) * math.log2(math.e), 0.0)
    row = lax.broadcasted_iota(jnp.int32, (l, l), 0)
    col = lax.broadcasted_iota(jnp.int32, (l, l), 1)
    causal = row >= col
    tril = jnp.where(causal, 1.0, 0.0).astype(BF16)
    lane = lax.broadcasted_iota(jnp.int32, (l, LANES), 1)

    def program(s):
        xbc = xbc_ref[s]
        xs = xbc[:, :SSD_D_INNER].astype(F32)
        dt = dt_ref[s]
        la = dt * a_scaled
        expand = expand_ref[...]
        acum = sum(jnp.dot(tril, part, preferred_element_type=F32) for part in _split3(la))
        yield
        acum_t = acum.T
        alast = acum[l - 1:l, :]
        dte = jnp.exp2(alast - acum)
        dt_x = jnp.dot(dt.astype(BF16), expand, preferred_element_type=F32)
        yield
        w_x = jnp.dot((dt * dte).astype(BF16), expand, preferred_element_type=F32)
        cdec = jnp.broadcast_to(jnp.exp2(alast), (SUBLANES, LANES))
        cdec_x = sum(jnp.dot(part, expand, preferred_element_type=F32) for part in _split3(cdec))[0:1]
        yield
        xd = (xs * dt_x).astype(BF16)
        xw = (xs * w_x).astype(BF16)
        yield
        pairs = []
        for g in range(SSD_GROUPS):
            bm = xbc[:, SSD_D_INNER + g * n:SSD_D_INNER + (g + 1) * n]
            cm = xbc[:, SSD_D_INNER + (SSD_GROUPS + g) * n:SSD_D_INNER + (SSD_GROUPS + g + 1) * n]
            cb = lax.dot_general(cm, bm, (((1,), (1,)), ((), ())), preferred_element_type=F32)
            cmf = cm.astype(F32)
            state = state_ref[s, g]
            state_b = state.astype(BF16)
            for pr in range(heads_per_group // 2):
                res = []
                for sub in range(2):
                    h = g * heads_per_group + 2 * pr + sub
                    a_col = jnp.broadcast_to(acum[:, h:h + 1], (l, l))
                    seg = a_col - acum_t[h:h + 1, :]
                    decay = jnp.exp2(jnp.where(causal, seg, -jnp.inf))
                    lhs = jnp.concatenate([(cb * decay).astype(BF16),
                                           (cmf * jnp.exp2(a_col)).astype(BF16)], axis=1)
                    c0 = g * gw + pr * LANES
                    rhs = jnp.concatenate([xd[:, c0:c0 + LANES],
                                           state_b[:, pr * LANES:(pr + 1) * LANES]], axis=0)
                    res.append(jnp.dot(lhs, rhs, preferred_element_type=F32))
                    yield
                pairs.append(jnp.where(lane < SSD_HEAD_DIM, res[0], res[1]))
            bm_t = bm.astype(F32).T.astype(BF16)
            new = jnp.dot(bm_t, xw[:, g * gw:(g + 1) * gw], preferred_element_type=F32)
            state_ref[s, g] = state * cdec_x[:, g * gw:(g + 1) * gw] + new
            yield
        y = jnp.concatenate(pairs, axis=1)
        y = y + dskip_ref[...] * xs
        y = y * _silu(z_ref[s].astype(F32))
        yield
        y = jnp.concatenate([_rms(y[:, g * gw:(g + 1) * gw], SSD_NORM_EPS)
                             for g in range(SSD_GROUPS)], axis=1)
        o_ref[s] = (y * nw_ref[...]).astype(BF16)

    programs = [program(s) for s in range(SSD_BATCH)]
    while programs:
        for p in list(programs):
            if next(p, programs) is programs:
                programs.remove(p)


def _ssd(xbc, dt, z, a_row, dskip_x, norm_w, bsz, seq):
    nc = seq // SSD_CHUNK
    head = jnp.arange(LANES)[:, None]
    chan = jnp.arange(SSD_D_INNER)[None, :] // SSD_HEAD_DIM
    expand = (head == chan).astype(BF16)

    def seq_spec(width):
        return pl.BlockSpec((SSD_BATCH, SSD_CHUNK, width), lambda b, c: (b, c, 0))

    out = pl.pallas_call(
        _ssd_kernel,
        grid=(bsz // SSD_BATCH, nc),
        in_specs=[
            seq_spec(SSD_CONV_DIM), seq_spec(LANES), seq_spec(SSD_D_INNER),
            _resident((1, LANES)), _resident((1, SSD_D_INNER)), _resident((1, SSD_D_INNER)),
            _resident((LANES, SSD_D_INNER)),
        ],
        out_specs=seq_spec(SSD_D_INNER),
        out_shape=jax.ShapeDtypeStruct((bsz, seq, SSD_D_INNER), BF16),
        scratch_shapes=[pltpu.VMEM((SSD_BATCH, SSD_GROUPS, SSD_D_STATE, SSD_D_INNER // SSD_GROUPS),
                                   F32)],
        compiler_params=pltpu.CompilerParams(dimension_semantics=("arbitrary", "arbitrary"),
                                             vmem_limit_bytes=VMEM_LIMIT_BYTES),
        name="ssd",
    )(xbc.reshape(bsz, seq, -1), dt.reshape(bsz, seq, -1), z.reshape(bsz, seq, -1),
      a_row, dskip_x, norm_w, expand)
    return out.reshape(bsz * seq, SSD_D_INNER)


def _attn_kernel(lq1_ref, lk1_ref, lq2_ref, lk2_ref, sw_ref, qt_ref, k_ref, vt_ref, o_ref,
                 st_ref, p_ref):
    hw = 2 * DA_HEAD_DIM
    seq = qt_ref.shape[1]
    t = ATT_BLOCK
    nq = seq // t
    lam = (jnp.exp(jnp.sum(lq1_ref[...] * lk1_ref[...], axis=-1, keepdims=True))
           - jnp.exp(jnp.sum(lq2_ref[...] * lk2_ref[...], axis=-1, keepdims=True)) + LAM_INIT)
    dim = lax.broadcasted_iota(jnp.int32, (hw, t), 0)
    query = lax.broadcasted_iota(jnp.int32, (t, 2 * t), 1)
    query = jnp.where(query >= t, query - t, query)
    key_le_query = lax.broadcasted_iota(jnp.int32, (t, 2 * t), 0) <= query

    def program(hd):
        hs = slice(hd * hw, (hd + 1) * hw)

        def masked_q(i):
            qt = qt_ref[hs, i * t:(i + 1) * t]
            zero = jnp.zeros_like(qt)
            return jnp.concatenate([jnp.where(dim < DA_HEAD_DIM, qt, zero),
                                    jnp.where(dim >= DA_HEAD_DIM, qt, zero)], axis=1)

        def score_piece(i, j, qm):
            st = jnp.dot(k_ref[j * t:(j + 1) * t, hs], qm, preferred_element_type=F32)
            if j == i:
                st = jnp.where(key_le_query, st, -jnp.inf)
            st_ref[hd, i % 2, j * t:(j + 1) * t, :] = st
            return jnp.max(st.reshape(t // SUBLANES, SUBLANES, 2 * t), axis=0)

        def softmax_piece(i, j, m):
            e = jnp.exp2(st_ref[hd, i % 2, j * t:(j + 1) * t, :] - m)
            p_ref[hd, i % 2, j * t:(j + 1) * t, :] = e.astype(BF16)
            return jnp.sum(e.reshape(t // SUBLANES, SUBLANES, 2 * t), axis=0)

        def pv_piece(i, j):
            return jnp.dot(vt_ref[hs, j * t:(j + 1) * t], p_ref[hd, i % 2, j * t:(j + 1) * t, :],
                           preferred_element_type=F32)

        def finish(i, acc, l):
            acc = acc / l
            d = acc[:, :t] - lam * acc[:, t:]
            d = d * lax.rsqrt(jnp.mean(d * d, axis=0, keepdims=True) + SUBLN_EPS)
            y = (d.T * sw_ref[...]) * (1.0 - LAM_INIT)
            o_ref[i * t:(i + 1) * t, hs] = y.astype(BF16)

        m = jnp.max(score_piece(0, 0, masked_q(0)), axis=0, keepdims=True)
        l_prev = None
        for i in range(nq + 1):
            n_score = i + 2 if i + 1 < nq else 0
            n_soft = i + 1 if i < nq else 0
            n_pv = i if i >= 1 else 0
            qm_next = masked_q(i + 1) if n_score else None
            m_next = None
            acc = None
            l = jnp.zeros((SUBLANES, 2 * t), F32)
            for j in range(max(n_score, n_soft, n_pv)):
                if j < n_score:
                    mj = score_piece(i + 1, j, qm_next)
                    m_next = mj if m_next is None else jnp.maximum(m_next, mj)
                if j < n_pv:
                    part = pv_piece(i - 1, j)
                    acc = part if acc is None else acc + part
                if j < n_soft:
                    l = l + softmax_piece(i, j, m)
                yield
            if n_pv:
                finish(i - 1, acc, l_prev)
            l_prev = jnp.sum(l, axis=0, keepdims=True)
            m = None if m_next is None else jnp.max(m_next, axis=0, keepdims=True)

    programs = [program(hd) for hd in range(ATT_HEADS)]
    while programs:
        for p in list(programs):
            if next(p, programs) is programs:
                programs.remove(p)


def _attn(qt, k, vt, lq1, lk1, lq2, lk2, subln_w, bsz, seq):
    tokens = k.shape[0]
    hw = ATT_HEADS * 2 * DA_HEAD_DIM
    lam_spec = _resident((1, DA_HEAD_DIM))
    transposed_spec = pl.BlockSpec((hw, seq), lambda b, h: (h, b))
    return pl.pallas_call(
        _attn_kernel,
        grid=(bsz, DA_HEADS // ATT_HEADS),
        in_specs=[
            lam_spec, lam_spec, lam_spec, lam_spec, _resident((1, 2 * DA_HEAD_DIM)),
            transposed_spec,
            pl.BlockSpec((seq, hw), lambda b, h: (b, h)),
            transposed_spec,
        ],
        out_specs=pl.BlockSpec((seq, hw), lambda b, h: (b, h)),
        out_shape=jax.ShapeDtypeStruct((tokens, DA_WIDTH), BF16),
        scratch_shapes=[pltpu.VMEM((ATT_HEADS, 2, seq, 2 * ATT_BLOCK), F32),
                        pltpu.VMEM((ATT_HEADS, 2, seq, 2 * ATT_BLOCK), BF16)],
        compiler_params=pltpu.CompilerParams(dimension_semantics=("arbitrary", "arbitrary"),
                                             vmem_limit_bytes=VMEM_LIMIT_BYTES),
        name="attn",
    )(lq1, lk1, lq2, lk2, subln_w, qt, k, vt)


def _merge_kernel(x_ref, ys_ref, ya_ref, g_ref, gate_ref, nw_ref, wso_ref, wao_ref, wout_ref, o_ref):
    d = x_ref.shape[1]
    so = jnp.dot(ys_ref[...], wso_ref[...], preferred_element_type=F32)
    ao = jnp.dot(ya_ref[...], wao_ref[...], preferred_element_type=F32)
    merged = g_ref[:, :d].astype(F32) * so + g_ref[:, d:].astype(F32) * ao
    mix = jnp.dot(merged.astype(BF16), wout_ref[...], preferred_element_type=F32)
    o_ref[...] = x_ref[...] + gate_ref[...] * (_rms(mix, NORM_EPS) * nw_ref[...])


def _merge(x2, y_ssd, y_att, gates, mod, post_norm_w, w_ssd_o, w_attn_o, w_out, seq):
    tokens, d = x2.shape
    tiles_per_seq = seq // ROW_TILE

    def row_spec(width):
        return pl.BlockSpec((ROW_TILE, width), lambda i: (i, 0))

    return pl.pallas_call(
        _merge_kernel,
        grid=(tokens // ROW_TILE,),
        in_specs=[
            row_spec(d), row_spec(d), row_spec(d), row_spec(2 * d),
            pl.BlockSpec((None, None, 1, d), lambda i: (2, i // tiles_per_seq, 0, 0)),
            _resident((1, d)), _resident((d, d)), _resident((d, d)), _resident((d, d)),
        ],
        out_specs=row_spec(d),
        out_shape=jax.ShapeDtypeStruct((tokens, d), F32),
        compiler_params=pltpu.CompilerParams(dimension_semantics=("arbitrary",),
                                             vmem_limit_bytes=VMEM_LIMIT_BYTES),
        name="merge",
    )(x2, y_ssd, y_att, gates, mod, post_norm_w, w_ssd_o, w_attn_o, w_out)


def _ffn_kernel(tiles_per_seq, x_ref, sc_ref, sh_ref, gate_ref, nw_ref, pw_ref, wup_ref, cw_ref,
                cb_ref, wdn_ref, o_ref, carry_ref, acc_ref, hb_ref, u_ref, act_ref):
    rows = x_ref.shape[0]
    width = 2 * FF_CHUNK
    n_chunks = D_FF // FF_CHUNK
    first = (pl.program_id(0) % tiles_per_seq) == 0
    x = x_ref[...]
    h = (_rms(x, NORM_EPS) * nw_ref[...]) * (1.0 + sc_ref[...]) + sh_ref[...]
    hb_ref[...] = h.astype(BF16)
    k0 = math.sqrt(2.0 / math.pi)

    def up_block(chunk, rb):
        r0 = rb * ROW_BLOCK
        for half, c0 in enumerate((chunk * FF_CHUNK, D_FF + chunk * FF_CHUNK)):
            u_ref[chunk % 2, SUBLANES + r0:SUBLANES + r0 + ROW_BLOCK,
                  half * FF_CHUNK:(half + 1) * FF_CHUNK] = jnp.dot(
                hb_ref[r0:r0 + ROW_BLOCK, :], wup_ref[:, c0:c0 + FF_CHUNK],
                preferred_element_type=F32)

    def gate_block(chunk, rb):
        r0 = rb * ROW_BLOCK
        slot = chunk % 2
        w = cw_ref[:, chunk * width:(chunk + 1) * width]
        y = cb_ref[:, chunk * width:(chunk + 1) * width]
        for k in range(FFN_CONV):
            lo = SUBLANES + r0 - k
            y = y + u_ref[slot, lo:lo + ROW_BLOCK, :] * w[FFN_CONV - 1 - k:FFN_CONV - k]
        ug = y[:, :FF_CHUNK]
        uv = y[:, FF_CHUNK:]
        act = 0.5 * ug * (1.0 + jnp.tanh(k0 * (ug + 0.044715 * (ug * ug * ug)))) * uv
        act_ref[slot, r0:r0 + ROW_BLOCK, :] = act.astype(BF16)

    def down_block(chunk, rb):
        r0 = rb * ROW_BLOCK
        part = jnp.dot(act_ref[chunk % 2, r0:r0 + ROW_BLOCK, :],
                       wdn_ref[chunk * FF_CHUNK:(chunk + 1) * FF_CHUNK, :],
                       preferred_element_type=F32)
        if chunk == 0:
            acc_ref[r0:r0 + ROW_BLOCK, :] = part
        else:
            acc_ref[r0:r0 + ROW_BLOCK, :] += part

    def load_halo(chunk):
        u_ref[chunk % 2, 0:SUBLANES, :] = jnp.where(
            first, 0.0, carry_ref[:, chunk * width:(chunk + 1) * width])

    def save_halo(chunk):
        carry_ref[:, chunk * width:(chunk + 1) * width] = u_ref[chunk % 2, rows:rows + SUBLANES, :]

    n_rb = rows // ROW_BLOCK
    load_halo(0)
    for rb in range(n_rb):
        up_block(0, rb)
    for chunk in range(n_chunks + 1):
        if chunk + 1 < n_chunks:
            load_halo(chunk + 1)
        for rb in range(n_rb):
            if chunk + 1 < n_chunks:
                up_block(chunk + 1, rb)
            if chunk >= 1:
                down_block(chunk - 1, rb)
            if chunk < n_chunks:
                gate_block(chunk, rb)
        if chunk < n_chunks:
            save_halo(chunk)

    o_ref[...] = x + gate_ref[...] * (_rms(acc_ref[...], NORM_EPS) * pw_ref[...])


def _chunk_major(a):
    lead = a.shape[:-1]
    g = a[..., :D_FF].reshape(*lead, D_FF // FF_CHUNK, FF_CHUNK)
    v = a[..., D_FF:].reshape(*lead, D_FF // FF_CHUNK, FF_CHUNK)
    return jnp.concatenate([g, v], axis=-1).reshape(*lead, 2 * D_FF)


def _ffn(x1, mod, pre_norm_w, post_norm_w, w_up, conv_w, conv_b, w_down, seq):
    tokens, d = x1.shape
    rows = FFN_ROW_TILE
    tiles_per_seq = seq // rows

    def row_spec(width):
        return pl.BlockSpec((rows, width), lambda i: (i, 0))

    def mod_spec(which):
        return pl.BlockSpec((None, None, 1, d), lambda i: (which, i // tiles_per_seq, 0, 0))

    return pl.pallas_call(
        functools.partial(_ffn_kernel, tiles_per_seq),
        grid=(tokens // rows,),
        in_specs=[
            row_spec(d), mod_spec(4), mod_spec(3), mod_spec(5), _resident((1, d)), _resident((1, d)),
            _resident((d, 2 * D_FF)), _resident((FFN_CONV, 2 * D_FF)), _resident((1, 2 * D_FF)),
            _resident((D_FF, d)),
        ],
        out_specs=row_spec(d),
        out_shape=jax.ShapeDtypeStruct((tokens, d), F32),
        scratch_shapes=[
            pltpu.VMEM((SUBLANES, 2 * D_FF), F32),
            pltpu.VMEM((rows, d), F32),
            pltpu.VMEM((rows, d), BF16),
            pltpu.VMEM((2, SUBLANES + rows, 2 * FF_CHUNK), F32),
            pltpu.VMEM((2, rows, FF_CHUNK), BF16),
        ],
        compiler_params=pltpu.CompilerParams(dimension_semantics=("arbitrary",),
                                             vmem_limit_bytes=VMEM_LIMIT_BYTES),
        name="ffn",
    )(x1, mod, mod, mod, pre_norm_w, post_norm_w, w_up, _chunk_major(conv_w), _chunk_major(conv_b),
      w_down)


def kernel(x, c, w_ada, b_ada, pre_norm1_w, w_in, conv_ssd_w, conv_ssd_b, dt_bias, a_log, d_skip,
           ssd_norm_w, w_ssd_o, lambda_q1, lambda_k1, lambda_q2, lambda_k2, subln_w, w_attn_o, w_out,
           post_norm1_w, pre_norm2_w, w_up, conv_ffn_w, conv_ffn_b, w_down, post_norm2_w):
    bsz, seq, d = x.shape
    assert d == D_MODEL and seq % ROW_TILE == 0 and seq % ATT_BLOCK == 0 and seq % SSD_CHUNK == 0
    assert seq % FFN_ROW_TILE == 0 and bsz % SSD_BATCH == 0
    assert w_ada.shape[0] == 1, "single-layer block"
    layer = 0
    x2 = x.reshape(bsz * seq, d)

    mod = _ada(c, w_ada[layer], b_ada[layer]).reshape(6, bsz, 1, d)

    w = w_in[layer]
    s1 = SSD_D_INNER + SSD_CONV_DIM
    s2 = s1 + SSD_HEADS
    w_ssd = w[:, :s1].astype(BF16)
    w_att = w[:, s2:].astype(BF16)
    w_dt = jnp.pad(w[:, s1:s2], ((0, 0), (0, LANES - SSD_HEADS))).astype(BF16)
    dt_bias_pad = jnp.pad(dt_bias[layer], (0, LANES - SSD_HEADS)).reshape(1, LANES)

    z, xbc, qt, k, vt, gates, dt = _inproj(
        x2, mod, pre_norm1_w[layer].reshape(1, d), w_ssd, w_att, w_dt, conv_ssd_w[layer],
        conv_ssd_b[layer].reshape(1, -1), dt_bias_pad, seq)

    a_row = jnp.pad(a_log[layer].astype(F32), (0, LANES - SSD_HEADS)).reshape(1, LANES)
    dskip_x = jnp.repeat(d_skip[layer].astype(F32), SSD_HEAD_DIM).reshape(1, SSD_D_INNER)
    y_ssd = _ssd(xbc, dt, z, a_row, dskip_x, ssd_norm_w[layer].reshape(1, -1), bsz, seq)

    y_att = _attn(qt, k, vt, lambda_q1[layer].reshape(1, -1), lambda_k1[layer].reshape(1, -1),
                  lambda_q2[layer].reshape(1, -1), lambda_k2[layer].reshape(1, -1),
                  subln_w[layer].reshape(1, -1), bsz, seq)

    x1 = _merge(x2, y_ssd, y_att, gates, mod, post_norm1_w[layer].reshape(1, d),
                w_ssd_o[layer].astype(BF16), w_attn_o[layer].astype(BF16), w_out[layer].astype(BF16),
                seq)

    out = _ffn(x1, mod, pre_norm2_w[layer].reshape(1, d), post_norm2_w[layer].reshape(1, d),
               w_up[layer].astype(BF16), conv_ffn_w[layer], conv_ffn_b[layer].reshape(1, -1),
               w_down[layer].astype(BF16), seq)
    return out.reshape(bsz, seq, d)
```

```python
import functools
import math

import jax
import jax.numpy as jnp
from jax import lax
from jax.experimental import pallas as pl
from jax.experimental.pallas import tpu as pltpu

F32 = jnp.float32
BF16 = jnp.bfloat16

D_MODEL = 1024
SSD_D_INNER = 1024
SSD_HEAD_DIM = 64
SSD_HEADS = SSD_D_INNER // SSD_HEAD_DIM
SSD_GROUPS = 2
SSD_D_STATE = 128
SSD_CONV = 4
SSD_CHUNK = 128
SSD_CONV_DIM = SSD_D_INNER + 2 * SSD_GROUPS * SSD_D_STATE
SSD_NORM_EPS = 1e-5
DA_HEADS = 8
DA_HEAD_DIM = 64
DA_WIDTH = DA_HEADS * 2 * DA_HEAD_DIM
ROPE_THETA = 500000.0
ROPE_DIM = DA_HEAD_DIM // 4
SUBLN_EPS = 1e-5
D_FF = 2816
FFN_CONV = 3
NORM_EPS = 1e-6
LAM_INIT = 0.8 - 0.6 * math.exp(-0.3 * 0)

LANES = 128
SUBLANES = 8
VMEM_LIMIT_BYTES = 56 * 1024 * 1024

ROW_TILE = 512
FFN_ROW_TILE = 512
COL_CHUNK = 512
FF_CHUNK = 256
ATT_BLOCK = 256
ROW_BLOCK = 128
SSD_BATCH = 2
ATT_HEADS = 2


def _silu(x):
    return x * jax.nn.sigmoid(x)


def _rms(x, eps):
    return x * lax.rsqrt(jnp.mean(x * x, axis=-1, keepdims=True) + eps)


def _resident(shape):
    nd = len(shape)
    return pl.BlockSpec(shape, lambda *_: (0,) * nd, pipeline_mode=pl.Buffered(1))


def _ada_kernel(c_ref, w_ref, b_ref, o_ref):
    c = c_ref[...]
    o_ref[0] = jnp.dot(_silu(c), w_ref[...], preferred_element_type=F32,
                       precision=lax.Precision.HIGHEST) + b_ref[0]


def _ada(c, w_ada, b_ada):
    bsz, d = c.shape
    n_mod = w_ada.shape[1] // d
    return pl.pallas_call(
        _ada_kernel,
        grid=(n_mod,),
        in_specs=[
            pl.BlockSpec((bsz, d), lambda j: (0, 0)),
            pl.BlockSpec((d, d), lambda j: (0, j)),
            pl.BlockSpec((1, 1, d), lambda j: (j, 0, 0)),
        ],
        out_specs=pl.BlockSpec((1, bsz, d), lambda j: (j, 0, 0)),
        out_shape=jax.ShapeDtypeStruct((n_mod, bsz, d), F32),
        compiler_params=pltpu.CompilerParams(dimension_semantics=("arbitrary",),
                                             vmem_limit_bytes=VMEM_LIMIT_BYTES),
        name="ada",
    )(c, w_ada, b_ada.reshape(n_mod, 1, d))


_OFF_Z = 0
_OFF_XBC = _OFF_Z + SSD_D_INNER
_OFF_Q = _OFF_XBC + SSD_CONV_DIM
_OFF_K = _OFF_Q + DA_WIDTH
_OFF_V = _OFF_K + DA_WIDTH
_OFF_G = _OFF_V + DA_WIDTH
_OFF_DT = _OFF_G + 2 * D_MODEL
_W_COLS = _OFF_DT + LANES


def _inproj_kernel(tiles_per_seq, x_ref, sc_ref, sh_ref, nw_ref, w_ref, cosf_ref, sina_ref,
                   sinb_ref, cw_ref, cb_ref, dtb_ref,
                   z_ref, xbc_ref, qt_ref, k_ref, vt_ref, g_ref, dt_ref, carry_ref, hb_ref, u_ref):
    rows = x_ref.shape[0]
    first = (pl.program_id(0) % tiles_per_seq) == 0
    x = x_ref[...]
    h = (_rms(x, NORM_EPS) * nw_ref[...]) * (1.0 + sc_ref[...]) + sh_ref[...]
    hb_ref[...] = h.astype(BF16)
    half = ROPE_DIM // 2
    q_scale = DA_HEAD_DIM ** -0.5 * math.log2(math.e)

    def rope(t, r0):
        rs = slice(r0, r0 + ROW_BLOCK)
        return (t * cosf_ref[rs, :] + pltpu.roll(t, LANES - half, axis=1) * sina_ref[rs, :]
                + pltpu.roll(t, half, axis=1) * sinb_ref[rs, :])

    def block(slot, r0, width=COL_CHUNK):
        return u_ref[slot, SUBLANES + r0:SUBLANES + r0 + ROW_BLOCK, :width]

    def epi_z(slot, r0, c):
        z_ref[r0:r0 + ROW_BLOCK, c:c + COL_CHUNK] = block(slot, r0).astype(BF16)

    def epi_xbc(slot, r0, c):
        w = cw_ref[:, c:c + COL_CHUNK]
        y = cb_ref[:, c:c + COL_CHUNK]
        for k in range(SSD_CONV):
            lo = SUBLANES + r0 - k
            y = y + u_ref[slot, lo:lo + ROW_BLOCK, :] * w[SSD_CONV - 1 - k:SSD_CONV - k]
        xbc_ref[r0:r0 + ROW_BLOCK, c:c + COL_CHUNK] = _silu(y).astype(BF16)

    def epi_q(slot, r0, c):
        blk = block(slot, r0)
        for s in range(0, COL_CHUNK, LANES):
            r = rope(blk[:, s:s + LANES], r0) * q_scale
            qt_ref[c + s:c + s + LANES, r0:r0 + ROW_BLOCK] = r.T.astype(BF16)

    def epi_k(slot, r0, c):
        blk = block(slot, r0)
        for s in range(0, COL_CHUNK, LANES):
            k_ref[r0:r0 + ROW_BLOCK, c + s:c + s + LANES] = rope(blk[:, s:s + LANES], r0).astype(BF16)

    def epi_v(slot, r0, c):
        vt_ref[c:c + COL_CHUNK, r0:r0 + ROW_BLOCK] = block(slot, r0).T.astype(BF16)

    def epi_g(slot, r0, c):
        g_ref[r0:r0 + ROW_BLOCK, c:c + COL_CHUNK] = jax.nn.sigmoid(block(slot, r0)).astype(BF16)

    def epi_dt(slot, r0, c):
        dt_raw = block(slot, r0, LANES) + dtb_ref[...]
        dt_ref[r0:r0 + ROW_BLOCK, :] = (jnp.maximum(dt_raw, 0.0)
                                        + jnp.log1p(jnp.exp(-jnp.abs(dt_raw))))

    stages = []
    for off, total, epi in ((_OFF_XBC, SSD_CONV_DIM, epi_xbc), (_OFF_Q, DA_WIDTH, epi_q),
                            (_OFF_K, DA_WIDTH, epi_k), (_OFF_V, DA_WIDTH, epi_v),
                            (_OFF_G, 2 * D_MODEL, epi_g), (_OFF_Z, SSD_D_INNER, epi_z)):
        stages += [(off + c, COL_CHUNK, epi, c) for c in range(0, total, COL_CHUNK)]
    stages.append((_OFF_DT, LANES, epi_dt, 0))

    def mm_block(s, rb):
        off, width, _, _ = stages[s]
        r0 = rb * ROW_BLOCK
        u_ref[s % 2, SUBLANES + r0:SUBLANES + r0 + ROW_BLOCK, :width] = jnp.dot(
            hb_ref[r0:r0 + ROW_BLOCK, :], w_ref[:, off:off + width], preferred_element_type=F32)

    def load_halo(s):
        _, _, epi, c = stages[s]
        if epi is epi_xbc:
            u_ref[s % 2, 0:SUBLANES, :] = jnp.where(first, 0.0, carry_ref[:, c:c + COL_CHUNK])

    def save_halo(s):
        _, _, epi, c = stages[s]
        if epi is epi_xbc:
            carry_ref[:, c:c + COL_CHUNK] = u_ref[s % 2, rows:rows + SUBLANES, :]

    n_rb = rows // ROW_BLOCK
    load_halo(0)
    for rb in range(n_rb):
        mm_block(0, rb)
    for s in range(len(stages)):
        if s + 1 < len(stages):
            load_halo(s + 1)
        for rb in range(n_rb):
            if s + 1 < len(stages):
                mm_block(s + 1, rb)
            stages[s][2](s % 2, rb * ROW_BLOCK, stages[s][3])
        save_halo(s)


def _rope_tables(seq):
    pos = jnp.arange(seq, dtype=F32)
    inv_freq = jnp.power(ROPE_THETA, -jnp.arange(0, ROPE_DIM, 2, dtype=F32) / ROPE_DIM)
    ang = pos[:, None] * inv_freq[None, :]
    cos, sin = jnp.cos(ang), jnp.sin(ang)
    half = ROPE_DIM // 2
    d = jnp.arange(LANES) % DA_HEAD_DIM
    idx = d % half
    cosf = jnp.where(d[None, :] < ROPE_DIM, cos[:, idx], 1.0)
    sina = jnp.where(d[None, :] < half, -sin[:, idx], 0.0)
    sinb = jnp.where((d[None, :] >= half) & (d[None, :] < ROPE_DIM), sin[:, idx], 0.0)
    return cosf.astype(F32), sina.astype(F32), sinb.astype(F32)


def _inproj(x2, mod, pre_norm_w, w_cat, conv_w, conv_b, dt_bias_pad, seq):
    tokens, d = x2.shape
    tiles_per_seq = seq // ROW_TILE
    cosf, sina, sinb = _rope_tables(seq)

    def row_spec(width):
        return pl.BlockSpec((ROW_TILE, width), lambda i: (i, 0))

    def mod_spec(which):
        return pl.BlockSpec((None, None, 1, d), lambda i: (which, i // tiles_per_seq, 0, 0))

    def rope_spec():
        return pl.BlockSpec((ROW_TILE, LANES), lambda i: (i % tiles_per_seq, 0))

    def transposed_spec():
        return pl.BlockSpec((DA_WIDTH, ROW_TILE), lambda i: (0, i))

    transposed = jax.ShapeDtypeStruct((DA_WIDTH, tokens), BF16)
    out_shapes = (
        jax.ShapeDtypeStruct((tokens, SSD_D_INNER), BF16),
        jax.ShapeDtypeStruct((tokens, SSD_CONV_DIM), BF16),
        transposed,
        jax.ShapeDtypeStruct((tokens, DA_WIDTH), BF16),
        transposed,
        jax.ShapeDtypeStruct((tokens, 2 * D_MODEL), BF16),
        jax.ShapeDtypeStruct((tokens, LANES), F32),
    )
    return pl.pallas_call(
        functools.partial(_inproj_kernel, tiles_per_seq),
        grid=(tokens // ROW_TILE,),
        in_specs=[
            row_spec(d), mod_spec(1), mod_spec(0), _resident((1, d)), _resident((d, _W_COLS)),
            rope_spec(), rope_spec(), rope_spec(),
            _resident((SSD_CONV, SSD_CONV_DIM)), _resident((1, SSD_CONV_DIM)), _resident((1, LANES)),
        ],
        out_specs=[transposed_spec() if s is transposed else row_spec(s.shape[1])
                   for s in out_shapes],
        out_shape=out_shapes,
        scratch_shapes=[
            pltpu.VMEM((SUBLANES, SSD_CONV_DIM), F32),
            pltpu.VMEM((ROW_TILE, d), BF16),
            pltpu.VMEM((2, SUBLANES + ROW_TILE, COL_CHUNK), F32),
        ],
        compiler_params=pltpu.CompilerParams(dimension_semantics=("arbitrary",),
                                             vmem_limit_bytes=VMEM_LIMIT_BYTES),
        name="inproj",
    )(x2, mod, mod, pre_norm_w, w_cat, cosf, sina, sinb, conv_w, conv_b, dt_bias_pad)


def _split3(a):
    hi = a.astype(BF16)
    r = a - hi.astype(F32)
    mid = r.astype(BF16)
    lo = (r - mid.astype(F32)).astype(BF16)
    return hi, mid, lo


def _ssd_kernel(xbc_ref, dt_ref, z_ref, a_ref, dskip_ref, nw_ref, expand_ref, o_ref, state_ref):
    l = SSD_CHUNK
    n = SSD_D_STATE
    gw = SSD_D_INNER // SSD_GROUPS
    heads_per_group = SSD_HEADS // SSD_GROUPS

    @pl.when(pl.program_id(1) == 0)
    def _():
        state_ref[...] = jnp.zeros_like(state_ref)

    is_head = lax.broadcasted_iota(jnp.int32, (1, LANES), 1) < SSD_HEADS
    a_scaled = jnp.where(is_head, -jnp.exp(a_ref[...]) * math.log2(math.e), 0.0)
    row = lax.broadcasted_iota(jnp.int32, (l, l), 0)
    col = lax.broadcasted_iota(jnp.int32, (l, l), 1)
    causal = row >= col
    tril = jnp.where(causal, 1.0, 0.0).astype(BF16)
    lane = lax.broadcasted_iota(jnp.int32, (l, LANES), 1)

    def program(s):
        xbc = xbc_ref[s]
        xs = xbc[:, :SSD_D_INNER].astype(F32)
        dt = dt_ref[s]
        la = dt * a_scaled
        expand = expand_ref[...]
        acum = sum(jnp.dot(tril, part, preferred_element_type=F32) for part in _split3(la))
        yield
        acum_t = acum.T
        alast = acum[l - 1:l, :]
        dte = jnp.exp2(alast - acum)
        dt_x = jnp.dot(dt.astype(BF16), expand, preferred_element_type=F32)
        yield
        w_x = jnp.dot((dt * dte).astype(BF16), expand, preferred_element_type=F32)
        cdec = jnp.broadcast_to(jnp.exp2(alast), (SUBLANES, LANES))
        cdec_x = sum(jnp.dot(part, expand, preferred_element_type=F32) for part in _split3(cdec))[0:1]
        yield
        xd = (xs * dt_x).astype(BF16)
        xw = (xs * w_x).astype(BF16)
        yield
        pairs = []
        for g in range(SSD_GROUPS):
            bm = xbc[:, SSD_D_INNER + g * n:SSD_D_INNER + (g + 1) * n]
            cm = xbc[:, SSD_D_INNER + (SSD_GROUPS + g) * n:SSD_D_INNER + (SSD_GROUPS + g + 1) * n]
            cb = lax.dot_general(cm, bm, (((1,), (1,)), ((), ())), preferred_element_type=F32)
            cmf = cm.astype(F32)
            state = state_ref[s, g]
            state_b = state.astype(BF16)
            for pr in range(heads_per_group // 2):
                res = []
                for sub in range(2):
                    h = g * heads_per_group + 2 * pr + sub
                    a_col = jnp.broadcast_to(acum[:, h:h + 1], (l, l))
                    seg = a_col - acum_t[h:h + 1, :]
                    decay = jnp.exp2(jnp.where(causal, seg, -jnp.inf))
                    lhs = jnp.concatenate([(cb * decay).astype(BF16),
                                           (cmf * jnp.exp2(a_col)).astype(BF16)], axis=1)
                    c0 = g * gw + pr * LANES
                    rhs = jnp.concatenate([xd[:, c0:c0 + LANES],
                                           state_b[:, pr * LANES:(pr + 1) * LANES]], axis=0)
                    res.append(jnp.dot(lhs, rhs, preferred_element_type=F32))
                    yield
                pairs.append(jnp.where(lane < SSD_HEAD_DIM, res[0], res[1]))
            bm_t = bm.astype(F32).T.astype(BF16)
            new = jnp.dot(bm_t, xw[:, g * gw:(g + 1) * gw], preferred_element_type=F32)
            state_ref[s, g] = state * cdec_x[:, g * gw:(g + 1) * gw] + new
            yield
        y = jnp.concatenate(pairs, axis=1)
        y = y + dskip_ref[...] * xs
        y = y * _silu(z_ref[s].astype(F32))
        yield
        y = jnp.concatenate([_rms(y[:, g * gw:(g + 1) * gw], SSD_NORM_EPS)
                             for g in range(SSD_GROUPS)], axis=1)
        o_ref[s] = (y * nw_ref[...]).astype(BF16)

    programs = [program(s) for s in range(SSD_BATCH)]
    while programs:
        for p in list(programs):
            if next(p, programs) is programs:
                programs.remove(p)


def _ssd(xbc, dt, z, a_row, dskip_x, norm_w, bsz, seq):
    nc = seq // SSD_CHUNK
    head = jnp.arange(LANES)[:, None]
    chan = jnp.arange(SSD_D_INNER)[None, :] // SSD_HEAD_DIM
    expand = (head == chan).astype(BF16)

    def seq_spec(width):
        return pl.BlockSpec((SSD_BATCH, SSD_CHUNK, width), lambda b, c: (b, c, 0))

    out = pl.pallas_call(
        _ssd_kernel,
        grid=(bsz // SSD_BATCH, nc),
        in_specs=[
            seq_spec(SSD_CONV_DIM), seq_spec(LANES), seq_spec(SSD_D_INNER),
            _resident((1, LANES)), _resident((1, SSD_D_INNER)), _resident((1, SSD_D_INNER)),
            _resident((LANES, SSD_D_INNER)),
        ],
        out_specs=seq_spec(SSD_D_INNER),
        out_shape=jax.ShapeDtypeStruct((bsz, seq, SSD_D_INNER), BF16),
        scratch_shapes=[pltpu.VMEM((SSD_BATCH, SSD_GROUPS, SSD_D_STATE, SSD_D_INNER // SSD_GROUPS),
                                   F32)],
        compiler_params=pltpu.CompilerParams(dimension_semantics=("arbitrary", "arbitrary"),
                                             vmem_limit_bytes=VMEM_LIMIT_BYTES),
        name="ssd",
    )(xbc.reshape(bsz, seq, -1), dt.reshape(bsz, seq, -1), z.reshape(bsz, seq, -1),
      a_row, dskip_x, norm_w, expand)
    return out.reshape(bsz * seq, SSD_D_INNER)


def _attn_kernel(lq1_ref, lk1_ref, lq2_ref, lk2_ref, sw_ref, qt_ref, k_ref, vt_ref, o_ref,
                 st_ref, p_ref):
    hw = 2 * DA_HEAD_DIM
    seq = qt_ref.shape[1]
    t = ATT_BLOCK
    nq = seq // t
    lam = (jnp.exp(jnp.sum(lq1_ref[...] * lk1_ref[...], axis=-1, keepdims=True))
           - jnp.exp(jnp.sum(lq2_ref[...] * lk2_ref[...], axis=-1, keepdims=True)) + LAM_INIT)
    dim = lax.broadcasted_iota(jnp.int32, (hw, t), 0)
    query = lax.broadcasted_iota(jnp.int32, (t, 2 * t), 1)
    query = jnp.where(query >= t, query - t, query)
    key_le_query = lax.broadcasted_iota(jnp.int32, (t, 2 * t), 0) <= query

    def program(hd):
        hs = slice(hd * hw, (hd + 1) * hw)

        def masked_q(i):
            qt = qt_ref[hs, i * t:(i + 1) * t]
            zero = jnp.zeros_like(qt)
            return jnp.concatenate([jnp.where(dim < DA_HEAD_DIM, qt, zero),
                                    jnp.where(dim >= DA_HEAD_DIM, qt, zero)], axis=1)

        def score_piece(i, j, qm):
            st = jnp.dot(k_ref[j * t:(j + 1) * t, hs], qm, preferred_element_type=F32)
            if j == i:
                st = jnp.where(key_le_query, st, -jnp.inf)
            st_ref[hd, i % 2, j * t:(j + 1) * t, :] = st
            return jnp.max(st.reshape(t // SUBLANES, SUBLANES, 2 * t), axis=0)

        def softmax_piece(i, j, m):
            e = jnp.exp2(st_ref[hd, i % 2, j * t:(j + 1) * t, :] - m)
            p_ref[hd, i % 2, j * t:(j + 1) * t, :] = e.astype(BF16)
            return jnp.sum(e.reshape(t // SUBLANES, SUBLANES, 2 * t), axis=0)

        def pv_piece(i, j):
            return jnp.dot(vt_ref[hs, j * t:(j + 1) * t], p_ref[hd, i % 2, j * t:(j + 1) * t, :],
                           preferred_element_type=F32)

        def finish(i, acc, l):
            acc = acc / l
            d = acc[:, :t] - lam * acc[:, t:]
            d = d * lax.rsqrt(jnp.mean(d * d, axis=0, keepdims=True) + SUBLN_EPS)
            y = (d.T * sw_ref[...]) * (1.0 - LAM_INIT)
            o_ref[i * t:(i + 1) * t, hs] = y.astype(BF16)

        m = jnp.max(score_piece(0, 0, masked_q(0)), axis=0, keepdims=True)
        l_prev = None
        for i in range(nq + 1):
            n_score = i + 2 if i + 1 < nq else 0
            n_soft = i + 1 if i < nq else 0
            n_pv = i if i >= 1 else 0
            qm_next = masked_q(i + 1) if n_score else None
            m_next = None
            acc = None
            l = jnp.zeros((SUBLANES, 2 * t), F32)
            for j in range(max(n_score, n_soft, n_pv)):
                if j < n_score:
                    mj = score_piece(i + 1, j, qm_next)
                    m_next = mj if m_next is None else jnp.maximum(m_next, mj)
                if j < n_pv:
                    part = pv_piece(i - 1, j)
                    acc = part if acc is None else acc + part
                if j < n_soft:
                    l = l + softmax_piece(i, j, m)
                yield
            if n_pv:
                finish(i - 1, acc, l_prev)
            l_prev = jnp.sum(l, axis=0, keepdims=True)
            m = None if m_next is None else jnp.max(m_next, axis=0, keepdims=True)

    programs = [program(hd) for hd in range(ATT_HEADS)]
    while programs:
        for p in list(programs):
            if next(p, programs) is programs:
                programs.remove(p)


def _attn(qt, k, vt, lq1, lk1, lq2, lk2, subln_w, bsz, seq):
    tokens = k.shape[0]
    hw = ATT_HEADS * 2 * DA_HEAD_DIM
    lam_spec = _resident((1, DA_HEAD_DIM))
    transposed_spec = pl.BlockSpec((hw, seq), lambda b, h: (h, b))
    return pl.pallas_call(
        _attn_kernel,
        grid=(bsz, DA_HEADS // ATT_HEADS),
        in_specs=[
            lam_spec, lam_spec, lam_spec, lam_spec, _resident((1, 2 * DA_HEAD_DIM)),
            transposed_spec,
            pl.BlockSpec((seq, hw), lambda b, h: (b, h)),
            transposed_spec,
        ],
        out_specs=pl.BlockSpec((seq, hw), lambda b, h: (b, h)),
        out_shape=jax.ShapeDtypeStruct((tokens, DA_WIDTH), BF16),
        scratch_shapes=[pltpu.VMEM((ATT_HEADS, 2, seq, 2 * ATT_BLOCK), F32),
                        pltpu.VMEM((ATT_HEADS, 2, seq, 2 * ATT_BLOCK), BF16)],
        compiler_params=pltpu.CompilerParams(dimension_semantics=("arbitrary", "arbitrary"),
                                             vmem_limit_bytes=VMEM_LIMIT_BYTES),
        name="attn",
    )(lq1, lk1, lq2, lk2, subln_w, qt, k, vt)


def _merge_kernel(x_ref, ys_ref, ya_ref, g_ref, gate_ref, nw_ref, wso_ref, wao_ref, wout_ref, o_ref):
    d = x_ref.shape[1]
    so = jnp.dot(ys_ref[...], wso_ref[...], preferred_element_type=F32)
    ao = jnp.dot(ya_ref[...], wao_ref[...], preferred_element_type=F32)
    merged = g_ref[:, :d].astype(F32) * so + g_ref[:, d:].astype(F32) * ao
    mix = jnp.dot(merged.astype(BF16), wout_ref[...], preferred_element_type=F32)
    o_ref[...] = x_ref[...] + gate_ref[...] * (_rms(mix, NORM_EPS) * nw_ref[...])


def _merge(x2, y_ssd, y_att, gates, mod, post_norm_w, w_ssd_o, w_attn_o, w_out, seq):
    tokens, d = x2.shape
    tiles_per_seq = seq // ROW_TILE

    def row_spec(width):
        return pl.BlockSpec((ROW_TILE, width), lambda i: (i, 0))

    return pl.pallas_call(
        _merge_kernel,
        grid=(tokens // ROW_TILE,),
        in_specs=[
            row_spec(d), row_spec(d), row_spec(d), row_spec(2 * d),
            pl.BlockSpec((None, None, 1, d), lambda i: (2, i // tiles_per_seq, 0, 0)),
            _resident((1, d)), _resident((d, d)), _resident((d, d)), _resident((d, d)),
        ],
        out_specs=row_spec(d),
        out_shape=jax.ShapeDtypeStruct((tokens, d), F32),
        compiler_params=pltpu.CompilerParams(dimension_semantics=("arbitrary",),
                                             vmem_limit_bytes=VMEM_LIMIT_BYTES),
        name="merge",
    )(x2, y_ssd, y_att, gates, mod, post_norm_w, w_ssd_o, w_attn_o, w_out)


def _ffn_kernel(tiles_per_seq, x_ref, sc_ref, sh_ref, gate_ref, nw_ref, pw_ref, wup_ref, cw_ref,
                cb_ref, wdn_ref, o_ref, carry_ref, acc_ref, hb_ref, u_ref, act_ref):
    rows = x_ref.shape[0]
    width = 2 * FF_CHUNK
    n_chunks = D_FF // FF_CHUNK
    first = (pl.program_id(0) % tiles_per_seq) == 0
    x = x_ref[...]
    h = (_rms(x, NORM_EPS) * nw_ref[...]) * (1.0 + sc_ref[...]) + sh_ref[...]
    hb_ref[...] = h.astype(BF16)
    k0 = math.sqrt(2.0 / math.pi)

    def up_block(chunk, rb):
        r0 = rb * ROW_BLOCK
        for half, c0 in enumerate((chunk * FF_CHUNK, D_FF + chunk * FF_CHUNK)):
            u_ref[chunk % 2, SUBLANES + r0:SUBLANES + r0 + ROW_BLOCK,
                  half * FF_CHUNK:(half + 1) * FF_CHUNK] = jnp.dot(
                hb_ref[r0:r0 + ROW_BLOCK, :], wup_ref[:, c0:c0 + FF_CHUNK],
                preferred_element_type=F32)

    def gate_block(chunk, rb):
        r0 = rb * ROW_BLOCK
        slot = chunk % 2
        w = cw_ref[:, chunk * width:(chunk + 1) * width]
        y = cb_ref[:, chunk * width:(chunk + 1) * width]
        for k in range(FFN_CONV):
            lo = SUBLANES + r0 - k
            y = y + u_ref[slot, lo:lo + ROW_BLOCK, :] * w[FFN_CONV - 1 - k:FFN_CONV - k]
        ug = y[:, :FF_CHUNK]
        uv = y[:, FF_CHUNK:]
        act = 0.5 * ug * (1.0 + jnp.tanh(k0 * (ug + 0.044715 * (ug * ug * ug)))) * uv
        act_ref[slot, r0:r0 + ROW_BLOCK, :] = act.astype(BF16)

    def down_block(chunk, rb):
        r0 = rb * ROW_BLOCK
        part = jnp.dot(act_ref[chunk % 2, r0:r0 + ROW_BLOCK, :],
                       wdn_ref[chunk * FF_CHUNK:(chunk + 1) * FF_CHUNK, :],
                       preferred_element_type=F32)
        if chunk == 0:
            acc_ref[r0:r0 + ROW_BLOCK, :] = part
        else:
            acc_ref[r0:r0 + ROW_BLOCK, :] += part

    def load_halo(chunk):
        u_ref[chunk % 2, 0:SUBLANES, :] = jnp.where(
            first, 0.0, carry_ref[:, chunk * width:(chunk + 1) * width])

    def save_halo(chunk):
        carry_ref[:, chunk * width:(chunk + 1) * width] = u_ref[chunk % 2, rows:rows + SUBLANES, :]

    n_rb = rows // ROW_BLOCK
    load_halo(0)
    for rb in range(n_rb):
        up_block(0, rb)
    for chunk in range(n_chunks + 1):
        if chunk + 1 < n_chunks:
            load_halo(chunk + 1)
        for rb in range(n_rb):
            if chunk + 1 < n_chunks:
                up_block(chunk + 1, rb)
            if chunk >= 1:
                down_block(chunk - 1, rb)
            if chunk < n_chunks:
                gate_block(chunk, rb)
        if chunk < n_chunks:
            save_halo(chunk)

    o_ref[...] = x + gate_ref[...] * (_rms(acc_ref[...], NORM_EPS) * pw_ref[...])


def _chunk_major(a):
    lead = a.shape[:-1]
    g = a[..., :D_FF].reshape(*lead, D_FF // FF_CHUNK, FF_CHUNK)
    v = a[..., D_FF:].reshape(*lead, D_FF // FF_CHUNK, FF_CHUNK)
    return jnp.concatenate([g, v], axis=-1).reshape(*lead, 2 * D_FF)


def _ffn(x1, mod, pre_norm_w, post_norm_w, w_up, conv_w, conv_b, w_down, seq):
    tokens, d = x1.shape
    rows = FFN_ROW_TILE
    tiles_per_seq = seq // rows

    def row_spec(width):
        return pl.BlockSpec((rows, width), lambda i: (i, 0))

    def mod_spec(which):
        return pl.BlockSpec((None, None, 1, d), lambda i: (which, i // tiles_per_seq, 0, 0))

    return pl.pallas_call(
        functools.partial(_ffn_kernel, tiles_per_seq),
        grid=(tokens // rows,),
        in_specs=[
            row_spec(d), mod_spec(4), mod_spec(3), mod_spec(5), _resident((1, d)), _resident((1, d)),
            _resident((d, 2 * D_FF)), _resident((FFN_CONV, 2 * D_FF)), _resident((1, 2 * D_FF)),
            _resident((D_FF, d)),
        ],
        out_specs=row_spec(d),
        out_shape=jax.ShapeDtypeStruct((tokens, d), F32),
        scratch_shapes=[
            pltpu.VMEM((SUBLANES, 2 * D_FF), F32),
            pltpu.VMEM((rows, d), F32),
            pltpu.VMEM((rows, d), BF16),
            pltpu.VMEM((2, SUBLANES + rows, 2 * FF_CHUNK), F32),
            pltpu.VMEM((2, rows, FF_CHUNK), BF16),
        ],
        compiler_params=pltpu.CompilerParams(dimension_semantics=("arbitrary",),
                                             vmem_limit_bytes=VMEM_LIMIT_BYTES),
        name="ffn",
    )(x1, mod, mod, mod, pre_norm_w, post_norm_w, w_up, _chunk_major(conv_w), _chunk_major(conv_b),
      w_down)


def kernel(x, c, w_ada, b_ada, pre_norm1_w, w_in, conv_ssd_w, conv_ssd_b, dt_bias, a_log, d_skip,
           ssd_norm_w, w_ssd_o, lambda_q1, lambda_k1, lambda_q2, lambda_k2, subln_w, w_attn_o, w_out,
           post_norm1_w, pre_norm2_w, w_up, conv_ffn_w, conv_ffn_b, w_down, post_norm2_w):
    bsz, seq, d = x.shape
    assert d == D_MODEL and seq % ROW_TILE == 0 and seq % ATT_BLOCK == 0 and seq % SSD_CHUNK == 0
    assert seq % FFN_ROW_TILE == 0 and bsz % SSD_BATCH == 0
    assert w_ada.shape[0] == 1, "single-layer block"
    layer = 0
    x2 = x.reshape(bsz * seq, d)

    mod = _ada(c, w_ada[layer], b_ada[layer]).reshape(6, bsz, 1, d)

    w = w_in[layer].astype(BF16)
    s1 = SSD_D_INNER + SSD_CONV_DIM
    s2 = s1 + SSD_HEADS
    w_cat = jnp.concatenate(
        [w[:, :s1], w[:, s2:], jnp.pad(w[:, s1:s2], ((0, 0), (0, LANES - SSD_HEADS)))], axis=1)
    dt_bias_pad = jnp.pad(dt_bias[layer], (0, LANES - SSD_HEADS)).reshape(1, LANES)

    z, xbc, qt, k, vt, gates, dt = _inproj(
        x2, mod, pre_norm1_w[layer].reshape(1, d), w_cat, conv_ssd_w[layer],
        conv_ssd_b[layer].reshape(1, -1), dt_bias_pad, seq)

    a_row = jnp.pad(a_log[layer].astype(F32), (0, LANES - SSD_HEADS)).reshape(1, LANES)
    dskip_x = jnp.repeat(d_skip[layer].astype(F32), SSD_HEAD_DIM).reshape(1, SSD_D_INNER)
    y_ssd = _ssd(xbc, dt, z, a_row, dskip_x, ssd_norm_w[layer].reshape(1, -1), bsz, seq)

    y_att = _attn(qt, k, vt, lambda_q1[layer].reshape(1, -1), lambda_k1[layer].reshape(1, -1),
                  lambda_q2[layer].reshape(1, -1), lambda_k2[layer].reshape(1, -1),
                  subln_w[layer].reshape(1, -1), bsz, seq)

    x1 = _merge(x2, y_ssd, y_att, gates, mod, post_norm1_w[layer].reshape(1, d),
                w_ssd_o[layer].astype(BF16), w_attn_o[layer].astype(BF16), w_out[layer].astype(BF16),
                seq)

    out = _ffn(x1, mod, pre_norm2_w[layer].reshape(1, d), post_norm2_w[layer].reshape(1, d),
               w_up[layer].astype(BF16), conv_ffn_w[layer], conv_ffn_b[layer].reshape(1, -1),
               w_down[layer].astype(BF16), seq)
    return out.reshape(bsz, seq, d)
```
